```python
import math
import jax, jax.numpy as jnp
from jax import lax
import numpy as np

D_MODEL = 1024
BATCH = 8
SEQ = 2048
DEPTH = 2
DEC_BATCH = 32
DEC_SEQ = 4
PAST_LEN = 8192
PAGE_SIZE = 128

N_EVEN = (DEPTH + 1) // 2
N_ODD = DEPTH // 2

H_M = 4
DK_M = 256
DV_M = 256
D_QKM = H_M * DK_M
D_M = H_M * DV_M
CONV_W = 4
F_BIAS_LO = 3.0
F_BIAS_HI = 6.0
H_R = 4
DK_R = 256
DV_R = 256
D_QKR = H_R * DK_R
D_R = H_R * DV_R
RET_THETA = 10000.0
H_C = 8
HD_C = 64
D_QKC = 2 * H_C * HD_C
D_C = H_C * 2 * HD_C
ROPE_THETA = 500000.0
ROPE_DIM = HD_C // 4
Q_BLOCK = 128
CHUNK = 64
EPS = 1e-6

SIZES_A = (2 * D_QKM, D_M, D_M, H_M, H_M, D_QKR, D_QKR, D_R, D_R)
PROJ_A = sum(SIZES_A)
SIZES_C = (D_QKC, D_QKC, D_C, D_C)
PROJ_C = sum(SIZES_C)

kernel_name = 'hybrid_mlstm_retention_diffattn_step'

F32 = jnp.float32


def _split_idx(sizes):
    return [int(s) for s in np.cumsum(sizes)[:-1]]


def rms_norm(x, g):
    xf = x.astype(F32)
    y = xf * lax.rsqrt(jnp.mean(xf * xf, axis=-1, keepdims=True) + EPS)
    return (y * g.astype(F32)).astype(x.dtype)


def rope(x, pos, rot_dim, theta):
    half = rot_dim // 2
    inv = jnp.power(theta, -jnp.arange(half, dtype=F32) / half)
    ang = pos.astype(F32)[:, None] * inv[None, :]
    cos = jnp.cos(ang)[None, :, None, :]
    sin = jnp.sin(ang)[None, :, None, :]
    xf = x.astype(F32)
    x1 = xf[..., :half]
    x2 = xf[..., half:rot_dim]
    out = jnp.concatenate([x1 * cos - x2 * sin, x1 * sin + x2 * cos, xf[..., rot_dim:]], axis=-1)
    return out.astype(x.dtype)


def _chunks(a, L):
    B, T = a.shape[:2]
    return jnp.moveaxis(a.reshape(B, T // L, L, *a.shape[2:]), 1, 0)


def _unchunk(a):
    NC, B, L = a.shape[:3]
    return jnp.moveaxis(a, 0, 1).reshape(B, NC * L, *a.shape[3:])


def mlstm_scan(q, k, v, ig, fg, C0, n0, m0):
    T = q.shape[1]
    L = CHUNK if T % CHUNK == 0 else T
    causal = jnp.tril(jnp.ones((L, L), dtype=bool))[None, :, :, None]
    xs = (_chunks((q * DK_M ** -0.5).astype(F32), L), _chunks(k.astype(F32), L),
          _chunks(v.astype(F32), L), _chunks(ig.astype(F32), L),
          _chunks(jax.nn.log_sigmoid(fg.astype(F32)), L))

    def step(carry, inp):
        C, n, m = carry
        q_, k_, v_, i_, lf = inp
        b = jnp.cumsum(lf, axis=1)
        inter = b + m[:, None, :]
        Dm = b[:, :, None, :] - b[:, None, :, :] + i_[:, None, :, :]
        Dm = jnp.where(causal, Dm, -jnp.inf)
        m_t = jnp.maximum(inter, jnp.max(Dm, axis=2))
        w = jnp.exp(Dm - m_t[:, :, None, :])
        s_inter = jnp.exp(inter - m_t)
        qk = jnp.einsum('bthd,bshd->btsh', q_, k_) * w
        num = (jnp.einsum('btsh,bshv->bthv', qk, v_)
               + s_inter[..., None] * jnp.einsum('bthd,bhdv->bthv', q_, C))
        den = jnp.sum(qk, axis=2) + s_inter * jnp.einsum('bthd,bhd->bth', q_, n)
        h = num / jnp.maximum(jnp.abs(den), jnp.exp(-m_t))[..., None]
        bL = b[:, -1]
        dec = bL[:, None, :] - b + i_
        m_new = jnp.maximum(bL + m, jnp.max(dec, axis=1))
        ws = jnp.exp(dec - m_new[:, None, :])
        sc = jnp.exp(bL + m - m_new)
        C_new = sc[..., None, None] * C + jnp.einsum('bshd,bshv->bhdv', k_ * ws[..., None], v_)
        n_new = sc[..., None] * n + jnp.einsum('bshd->bhd', k_ * ws[..., None])
        return (C_new, n_new, m_new), h

    (C, n, m), hs = lax.scan(step, (C0.astype(F32), n0.astype(F32), m0.astype(F32)), xs)
    return _unchunk(hs), C, n, m


def retention_scan(q, k, v, S0):
    T = q.shape[1]
    L = CHUNK if T % CHUNK == 0 else T
    lg = jnp.log(1.0 - jnp.power(2.0, -5.0 - jnp.arange(H_R, dtype=F32)))
    j = jnp.arange(L, dtype=F32)
    rel = j[:, None] - j[None, :]
    decay = jnp.where((rel >= 0)[:, :, None], jnp.exp(rel[:, :, None] * lg), 0.0)
    q_in = jnp.exp((j + 1.0)[:, None] * lg)[None, :, :, None]
    k_out = jnp.exp((L - 1.0 - j)[:, None] * lg)[None, :, :, None]
    c_dec = jnp.exp(L * lg)[None, :, None, None]
    xs = (_chunks(q.astype(F32), L), _chunks(k.astype(F32), L), _chunks(v.astype(F32), L))

    def step(S, inp):
        q_, k_, v_ = inp
        qk = jnp.einsum('bthd,bshd->btsh', q_, k_) * decay
        o = jnp.einsum('btsh,bshv->bthv', qk, v_) + jnp.einsum('bthd,bhdv->bthv', q_ * q_in, S)
        S_new = c_dec * S + jnp.einsum('bshd,bshv->bhdv', k_ * k_out, v_)
        return S_new, o

    S, os_ = lax.scan(step, S0.astype(F32), xs)
    return _unchunk(os_), S


def even_mixer(h, pos0, C0, n0, m0, buf0, S0, w_in, b_i, b_f, cw, cb, gn_m, gn_r, w_out):
    B, T, _ = h.shape
    p = h @ w_in
    qk_m, v_m, z_m, ig, fg, q_r, k_r, v_r, z_r = jnp.split(p, _split_idx(SIZES_A), axis=-1)
    xpad = jnp.concatenate([buf0.astype(qk_m.dtype), qk_m], axis=1)
    acc = cb + xpad[:, 0:T] * cw[0]
    for j in range(1, CONV_W):
        acc = acc + xpad[:, j:j + T] * cw[j]
    qk_c = jax.nn.silu(acc)
    q_m = qk_c[..., :D_QKM].reshape(B, T, H_M, DK_M)
    k_m = qk_c[..., D_QKM:].reshape(B, T, H_M, DK_M)
    hm, C, n, m = mlstm_scan(q_m, k_m, v_m.reshape(B, T, H_M, DV_M),
                             ig.astype(F32) + b_i, fg.astype(F32) + b_f, C0, n0, m0)
    hm = rms_norm(hm.astype(h.dtype), gn_m) * jax.nn.silu(z_m).reshape(B, T, H_M, DV_M)
    pos = pos0 + jnp.arange(T)
    q_r = rope(q_r.reshape(B, T, H_R, DK_R), pos, DK_R, RET_THETA)
    k_r = rope(k_r.reshape(B, T, H_R, DK_R), pos, DK_R, RET_THETA) * DK_R ** -0.5
    hr, S = retention_scan(q_r, k_r, v_r.reshape(B, T, H_R, DV_R), S0)
    hr = rms_norm(hr.astype(h.dtype), gn_r) * jax.nn.silu(z_r).reshape(B, T, H_R, DV_R)
    out = jnp.concatenate([hm.reshape(B, T, D_M), hr.reshape(B, T, D_R)], axis=-1) @ w_out
    return out.astype(h.dtype), (C, n, m, xpad[:, T:], S)


def _diff_combine(s, lam):
    p = jax.nn.softmax(s, axis=-1)
    B, _, Q, K = p.shape
    p = p.reshape(B, H_C, 2, Q, K)
    return p[:, :, 0] - lam * p[:, :, 1]


def diff_attn_prompt(o, q, k, v, lam):
    B, T = q.shape[:2]
    nb = T // Q_BLOCK
    kpos = jnp.arange(T)
    vf = v.astype(F32)
    qb = _chunks(q, Q_BLOCK)

    def block(args):
        qi, bi = args
        s = jnp.einsum('bqhd,bkhd->bhqk', qi, k).astype(F32) * HD_C ** -0.5
        qpos = bi * Q_BLOCK + jnp.arange(Q_BLOCK)
        s = jnp.where((kpos[None, :] <= qpos[:, None])[None, None], s, -jnp.inf)
        a = _diff_combine(s, lam)
        return jnp.einsum('bhqk,bkhe->bqhe', a, vf)

    out = lax.map(block, (qb, jnp.arange(nb)))
    return _unchunk(out)


def diff_attn_cached(q, k, v, kp, vp, lam):
    Q = q.shape[1]
    P = kp.shape[1]
    scale = HD_C ** -0.5
    s_p = jnp.einsum('bqhd,bkhd->bhqk', q, kp).astype(F32) * scale
    s_n = jnp.einsum('bqhd,bkhd->bhqk', q, k).astype(F32) * scale
    s_n = jnp.where(jnp.tril(jnp.ones((Q, Q), dtype=bool))[None, None], s_n, -jnp.inf)
    a = _diff_combine(jnp.concatenate([s_p, s_n], axis=-1), lam)
    return (jnp.einsum('bhqk,bkhe->bqhe', a[..., :P], vp.astype(F32))
            + jnp.einsum('bhqk,bkhe->bqhe', a[..., P:], v.astype(F32)))


def odd_mixer(h, pos0, layer_idx, o, attend, w_in, lam_p, gn, w_out):
    B, T, _ = h.shape
    p = h @ w_in
    q, k, v, z = jnp.split(p, _split_idx(SIZES_C), axis=-1)
    pos = pos0 + jnp.arange(T)
    q = rope(q.reshape(B, T, 2 * H_C, HD_C), pos, ROPE_DIM, ROPE_THETA)
    k = rope(k.reshape(B, T, 2 * H_C, HD_C), pos, ROPE_DIM, ROPE_THETA)
    v = v.reshape(B, T, H_C, 2 * HD_C)
    lam_init = 0.8 - 0.6 * math.exp(-0.3 * layer_idx)
    lf = lam_p.astype(F32)
    lam = jnp.exp(jnp.sum(lf[0] * lf[1])) - jnp.exp(jnp.sum(lf[2] * lf[3])) + lam_init
    a = attend(o, q, k, v, lam)
    a = rms_norm(a.astype(h.dtype), gn) * (1.0 - lam_init)
    out = (a.reshape(B, T, D_C) * jax.nn.silu(z)) @ w_out
    return out.astype(h.dtype), k, v


def forward(x, pos0, st_C, st_n, st_m, st_conv, st_S, attend, norm_pre, norm_post,
            w_in_a, b_gate_i, b_gate_f, conv_w, conv_b, gn_mlstm, gn_ret, w_out_a,
            w_in_c, lambda_qk, gn_diff, w_out_c):
    Cs, ns, ms, bufs, Ss, ks, vs = [], [], [], [], [], [], []
    for i in range(DEPTH):
        h = rms_norm(x, norm_pre[i])
        if i % 2 == 0:
            e = i // 2
            out, (C, n, m, buf, S) = even_mixer(
                h, pos0, st_C[e], st_n[e], st_m[e], st_conv[e], st_S[e], w_in_a[e],
                b_gate_i[e], b_gate_f[e], conv_w[e], conv_b[e], gn_mlstm[e], gn_ret[e], w_out_a[e])
            Cs.append(C); ns.append(n); ms.append(m); bufs.append(buf); Ss.append(S)
        else:
            o = i // 2
            out, k, v = odd_mixer(h, pos0, i, o, attend, w_in_c[o], lambda_qk[o], gn_diff[o], w_out_c[o])
            ks.append(k); vs.append(v)
        x = x + rms_norm(out, norm_post[i])
    return (x, jnp.stack(Cs), jnp.stack(ns), jnp.stack(ms), jnp.stack(bufs), jnp.stack(Ss),
            jnp.stack(ks), jnp.stack(vs))


def setup_inputs(seed: int = 0) -> dict:
    key = jax.random.key(seed)
    ks = jax.random.split(key, 26)
    n_pages = PAST_LEN // PAGE_SIZE
    n_used = DEC_BATCH * n_pages
    n_pool = n_used + n_used // 4

    def nrm(k, shape, s):
        return jax.random.normal(k, shape, F32) * s

    return {
        'x_prompt': nrm(ks[0], (BATCH, SEQ, D_MODEL), 1.0),
        'x_sample': nrm(ks[1], (DEC_BATCH, DEC_SEQ, D_MODEL), 1.0),
        'state_mlstm_C': nrm(ks[2], (N_EVEN, DEC_BATCH, H_M, DK_M, DV_M), 0.5),
        'state_mlstm_n': nrm(ks[3], (N_EVEN, DEC_BATCH, H_M, DK_M), 0.5),
        'state_mlstm_m': nrm(ks[4], (N_EVEN, DEC_BATCH, H_M), 1.0),
        'state_mlstm_conv': nrm(ks[5], (N_EVEN, DEC_BATCH, CONV_W - 1, 2 * D_QKM), 1.0),
        'state_ret_S': nrm(ks[6], (N_EVEN, DEC_BATCH, H_R, DK_R, DV_R), 1.0),
        'cache_k': nrm(ks[7], (N_ODD, n_pool, PAGE_SIZE, 2 * H_C, HD_C), 1.0),
        'cache_v': nrm(ks[8], (N_ODD, n_pool, PAGE_SIZE, H_C, 2 * HD_C), 1.0),
        'page_table': jax.random.permutation(ks[9], n_pool)[:n_used].reshape(DEC_BATCH, n_pages).astype(jnp.int32),
        'norm_pre': 1.0 + nrm(ks[10], (DEPTH, D_MODEL), 0.05),
        'norm_post': 1.0 + nrm(ks[11], (DEPTH, D_MODEL), 0.05),
        'w_in_a': nrm(ks[12], (N_EVEN, D_MODEL, PROJ_A), D_MODEL ** -0.5),
        'b_gate_i': nrm(ks[13], (N_EVEN, H_M), 0.1),
        'b_gate_f': jnp.linspace(F_BIAS_LO, F_BIAS_HI, H_M, dtype=F32)[None, :] + nrm(ks[14], (N_EVEN, H_M), 0.1),
        'conv_w': nrm(ks[15], (N_EVEN, CONV_W, 2 * D_QKM), CONV_W ** -0.5),
        'conv_b': nrm(ks[16], (N_EVEN, 2 * D_QKM), 0.02),
        'gn_mlstm': 1.0 + nrm(ks[17], (N_EVEN, H_M, DV_M), 0.05),
        'gn_ret': 1.0 + nrm(ks[18], (N_EVEN, H_R, DV_R), 0.05),
        'w_out_a': nrm(ks[19], (N_EVEN, D_M + D_R, D_MODEL), (D_M + D_R) ** -0.5),
        'w_in_c': nrm(ks[20], (N_ODD, D_MODEL, PROJ_C), D_MODEL ** -0.5),
        'lambda_qk': nrm(ks[21], (N_ODD, 4, HD_C), 0.1),
        'gn_diff': 1.0 + nrm(ks[22], (N_ODD, H_C, 2 * HD_C), 0.05),
        'w_out_c': nrm(ks[23], (N_ODD, D_C, D_MODEL), D_C ** -0.5),
    }


def reference(x_prompt, x_sample, state_mlstm_C, state_mlstm_n, state_mlstm_m, state_mlstm_conv,
              state_ret_S, cache_k, cache_v, page_table, norm_pre, norm_post, w_in_a, b_gate_i,
              b_gate_f, conv_w, conv_b, gn_mlstm, gn_ret, w_out_a, w_in_c, lambda_qk, gn_diff, w_out_c):
    weights = (norm_pre, norm_post, w_in_a, b_gate_i, b_gate_f, conv_w, conv_b, gn_mlstm, gn_ret,
               w_out_a, w_in_c, lambda_qk, gn_diff, w_out_c)
    Bp = x_prompt.shape[0]
    (y_prompt, C_p, n_p, m_p, conv_p, S_p, k_p, v_p) = forward(
        x_prompt, 0,
        jnp.zeros((N_EVEN, Bp, H_M, DK_M, DV_M), F32),
        jnp.zeros((N_EVEN, Bp, H_M, DK_M), F32),
        jnp.zeros((N_EVEN, Bp, H_M), F32),
        jnp.zeros((N_EVEN, Bp, CONV_W - 1, 2 * D_QKM), x_prompt.dtype),
        jnp.zeros((N_EVEN, Bp, H_R, DK_R, DV_R), F32),
        diff_attn_prompt, *weights)

    def sample_attend(o, q, k, v, lam):
        Bs = q.shape[0]
        past = page_table.shape[1] * cache_k.shape[2]
        kp = cache_k[o][page_table].reshape(Bs, past, 2 * H_C, HD_C)
        vp = cache_v[o][page_table].reshape(Bs, past, H_C, 2 * HD_C)
        return diff_attn_cached(q, k, v, kp, vp, lam)

    pos_s = page_table.shape[1] * cache_k.shape[2]
    (y_sample, C_s, n_s, m_s, conv_s, S_s, k_s, v_s) = forward(
        x_sample, pos_s, state_mlstm_C, state_mlstm_n, state_mlstm_m, state_mlstm_conv,
        state_ret_S, sample_attend, *weights)
    return (y_prompt, y_sample, C_p, n_p, m_p, conv_p, S_p, k_p, v_p,
            C_s, n_s, m_s, conv_s, S_s, k_s, v_s)
```

```python
import functools
import math

import jax
import jax.numpy as jnp
from jax import lax
from jax.experimental import pallas as pl
from jax.experimental.pallas import tpu as pltpu

F32 = jnp.float32
BF16 = jnp.bfloat16

D_MODEL = 1024
H_M, DK_M, DV_M = 4, 256, 256
D_QKM = H_M * DK_M
D_M = H_M * DV_M
CONV_W = 4
H_R, DK_R, DV_R = 4, 256, 256
D_QKR = H_R * DK_R
D_R = H_R * DV_R
RET_THETA = 10000.0
H_C, HD_C = 8, 64
D_QKC = 2 * H_C * HD_C
D_C = H_C * 2 * HD_C
ROPE_THETA = 500000.0
ROPE_DIM = HD_C // 4
CHUNK = 64
EPS = 1e-6
PROJ_MAIN_A = 2 * D_QKM + 2 * D_M + 2 * D_QKR + 2 * D_R
LANES = 128
VMEM_LIMIT = 52 * 1024 * 1024

OFF_QKM, OFF_VM, OFF_ZM = 0, 2 * D_QKM, 2 * D_QKM + D_M
OFF_QR = 2 * D_QKM + 2 * D_M
OFF_KR, OFF_VR, OFF_ZR = OFF_QR + D_QKR, OFF_QR + 2 * D_QKR, OFF_QR + 2 * D_QKR + D_R


def _params(sem):
    return pltpu.CompilerParams(dimension_semantics=sem, vmem_limit_bytes=VMEM_LIMIT)


def _rms(x, g):
    return x * lax.rsqrt(jnp.mean(x * x, axis=-1, keepdims=True) + EPS) * g


def _silu(x):
    return x * jax.nn.sigmoid(x)


def _dot(a, b):
    return jnp.dot(a, b, preferred_element_type=F32)


def _dot_nt(a, b):
    return lax.dot_general(a, b, (((1,), (1,)), ((), ())), preferred_element_type=F32)


def _dot_tn(a, b):
    return lax.dot_general(a, b, (((0,), (0,)), ((), ())), preferred_element_type=F32)


def _in_proj_a_kernel(x_ref, g_ref, w_ref, wg_ref, o_ref, og_ref, xn_ref):
    @pl.when(pl.program_id(1) == 0)
    def _():
        xn = _rms(x_ref[...], g_ref[...]).astype(BF16)
        xn_ref[...] = xn
        og_ref[...] = _dot(xn, wg_ref[...])

    o_ref[...] = _dot(xn_ref[...], w_ref[...]).astype(o_ref.dtype)


def _in_proj_a(x, g, w_main, w_gate, out_dtype):
    M, K = x.shape
    N = w_main.shape[1]
    tm = min(M, 1024)
    tn = 1024
    return pl.pallas_call(
        _in_proj_a_kernel,
        grid=(M // tm, N // tn),
        in_specs=[
            pl.BlockSpec((tm, K), lambda i, j: (i, 0)),
            pl.BlockSpec((1, K), lambda i, j: (0, 0)),
            pl.BlockSpec((K, tn), lambda i, j: (0, j)),
            pl.BlockSpec((K, LANES), lambda i, j: (0, 0)),
        ],
        out_specs=[
            pl.BlockSpec((tm, tn), lambda i, j: (i, j)),
            pl.BlockSpec((tm, LANES), lambda i, j: (i, 0)),
        ],
        out_shape=[
            jax.ShapeDtypeStruct((M, N), out_dtype),
            jax.ShapeDtypeStruct((M, LANES), F32),
        ],
        scratch_shapes=[pltpu.VMEM((tm, K), BF16)],
        compiler_params=_params(("parallel", "arbitrary")),
        name="in_proj_even",
    )(x, g, w_main, w_gate)


def _even_mixer_kernel(p_ref, g_ref, cos_ref, sin_ref, gb_ref, cw_ref, cb_ref, gnm_ref, gnr_ref,
                       dec_ref, rtab_ref, cdec_ref, C0_ref, n0_ref, m0_ref, cv0_ref, S0_ref,
                       h_ref, C_ref, n_ref, m_ref, cv_ref, S_ref, xbuf, qkc, qkr, *, TB, L):
    t = pl.program_id(1)
    HIST = 8

    @pl.when(t == 0)
    def _():
        C_ref[...] = C0_ref[...]
        n_ref[...] = n0_ref[...]
        m_ref[...] = m0_ref[...]
        S_ref[...] = S0_ref[...]
        xbuf[HIST - (CONV_W - 1):HIST, :] = cv0_ref[0]

    @pl.when(t > 0)
    def _():
        xbuf[0:HIST, :] = xbuf[TB:TB + HIST, :]

    for cg in range(2 * D_QKM // 256):
        cs = slice(cg * 256, (cg + 1) * 256)
        x = p_ref[0, :, cs].astype(F32)
        xbuf[HIST:HIST + TB, cs] = x
        acc = cb_ref[:, cs] + xbuf[HIST - 3:HIST - 3 + TB, cs] * cw_ref[0:1, cs]
        acc = acc + xbuf[HIST - 2:HIST - 2 + TB, cs] * cw_ref[1:2, cs]
        acc = acc + xbuf[HIST - 1:HIST - 1 + TB, cs] * cw_ref[2:3, cs]
        acc = acc + x * cw_ref[3:4, cs]
        y = _silu(acc)
        if cg < D_QKM // 256:
            y = y * (DK_M ** -0.5)
        qkc[0:TB, cs] = y
    cv_ref[0] = xbuf[HIST + TB - (CONV_W - 1):HIST + TB, :]

    cos = cos_ref[...]
    sin = sin_ref[...]
    half = DK_R // 2
    for which, off, scale in ((0, OFF_QR, 1.0), (1, OFF_KR, DK_R ** -0.5)):
        for h in range(H_R):
            x1 = p_ref[0, :, off + h * DK_R:off + h * DK_R + half].astype(F32)
            x2 = p_ref[0, :, off + h * DK_R + half:off + (h + 1) * DK_R].astype(F32)
            base = which * D_QKR + h * DK_R
            qkr[0:TB, base:base + half] = (x1 * cos - x2 * sin) * scale
            qkr[0:TB, base + half:base + DK_R] = (x1 * sin + x2 * cos) * scale

    def chunk(rows):
        gates = g_ref[0, rows, :] + gb_ref[...]
        ti = lax.broadcasted_iota(jnp.int32, (L, L), 0)
        si = lax.broadcasted_iota(jnp.int32, (L, L), 1)
        eye = ti == si
        low = si <= ti
        upp = ti <= si
        for h in range(H_M):
            ig = gates[:, h:h + 1]
            lf = jax.nn.log_sigmoid(gates[:, H_M + h:H_M + h + 1])
            lf_r = jnp.sum(jnp.where(eye, lf, 0.0), axis=0, keepdims=True)
            i_r = jnp.sum(jnp.where(eye, ig, 0.0), axis=0, keepdims=True)
            b_c = jnp.sum(jnp.where(low, lf_r, 0.0), axis=1, keepdims=True)
            b_r = jnp.sum(jnp.where(upp, lf, 0.0), axis=0, keepdims=True)
            m_prev = m_ref[0, h:h + 1, 0:1]
            inter = b_c + m_prev
            Dm = jnp.where(low, b_c - b_r + i_r, -jnp.inf)
            m_t = jnp.maximum(inter, jnp.max(Dm, axis=1, keepdims=True))
            w = jnp.exp(Dm - m_t)
            s_int = jnp.exp(inter - m_t)
            q = qkc[rows, h * DK_M:(h + 1) * DK_M]
            k = qkc[rows, D_QKM + h * DK_M:D_QKM + (h + 1) * DK_M]
            vb = p_ref[0, rows, OFF_VM + h * DV_M:OFF_VM + (h + 1) * DV_M].astype(BF16)
            qb = q.astype(BF16)
            qk = _dot_nt(qb, k.astype(BF16)) * w
            C = C_ref[0, h]
            nvec = n_ref[0, h:h + 1, :]
            num = _dot(qk.astype(BF16), vb) + s_int * _dot(qb, C.astype(BF16))
            den = (jnp.sum(qk, axis=1, keepdims=True)
                   + s_int * jnp.sum(q * nvec, axis=1, keepdims=True))
            hh = num / jnp.maximum(jnp.abs(den), jnp.exp(-m_t))
            y = _rms(hh, gnm_ref[:, h * DV_M:(h + 1) * DV_M])
            z = p_ref[0, rows, OFF_ZM + h * DV_M:OFF_ZM + (h + 1) * DV_M].astype(F32)
            h_ref[0, rows, h * DV_M:(h + 1) * DV_M] = (y * _silu(z)).astype(h_ref.dtype)
            bL = b_c[L - 1:L, :]
            dec = bL - b_c + ig
            m_new = jnp.maximum(bL + m_prev, jnp.max(dec, axis=0, keepdims=True))
            ws = jnp.exp(dec - m_new)
            sc = jnp.exp(bL + m_prev - m_new)
            kw = k * ws
            C_ref[0, h] = sc * C + _dot_tn(kw.astype(BF16), vb)
            n_ref[0, h:h + 1, :] = sc * nvec + jnp.sum(kw, axis=0, keepdims=True)
            m_ref[0, h:h + 1, :] = jnp.broadcast_to(m_new, (1, LANES))
        for h in range(H_R):
            q = qkr[rows, h * DK_R:(h + 1) * DK_R]
            k = qkr[rows, D_QKR + h * DK_R:D_QKR + (h + 1) * DK_R]
            vb = p_ref[0, rows, OFF_VR + h * DV_R:OFF_VR + (h + 1) * DV_R].astype(BF16)
            qk = _dot_nt(q.astype(BF16), k.astype(BF16)) * dec_ref[h]
            S = S_ref[0, h]
            q_in = rtab_ref[:, h:h + 1]
            k_out = rtab_ref[:, H_R + h:H_R + h + 1]
            o = _dot(qk.astype(BF16), vb) + _dot((q * q_in).astype(BF16), S.astype(BF16))
            S_ref[0, h] = cdec_ref[:, h:h + 1] * S + _dot_tn((k * k_out).astype(BF16), vb)
            y = _rms(o, gnr_ref[:, h * DV_R:(h + 1) * DV_R])
            z = p_ref[0, rows, OFF_ZR + h * DV_R:OFF_ZR + (h + 1) * DV_R].astype(F32)
            h_ref[0, rows, D_M + h * DV_R:D_M + (h + 1) * DV_R] = (y * _silu(z)).astype(h_ref.dtype)

    n_chunks = TB // L
    if n_chunks == 1:
        chunk(slice(0, L))
    else:
        def body(c, carry):
            chunk(pl.ds(pl.multiple_of(c * L, L), L))
            return carry
        lax.fori_loop(0, n_chunks, body, 0)


def _even_mixer(p, gates, cos_r, sin_r, consts, C0, n0, m0, cv0, S0, TB, L):
    B, T, _ = p.shape
    gbias, cw, cb, gnm, gnr, decay, rtab, cdec = consts
    nt = T // TB
    tb8 = -(-TB // 8) * 8
    full = lambda shape: pl.BlockSpec(shape, lambda b, t: (0,) * len(shape))
    per_b = lambda shape: pl.BlockSpec((1,) + shape, lambda b, t: (b,) + (0,) * len(shape))
    kern = functools.partial(_even_mixer_kernel, TB=TB, L=L)
    return pl.pallas_call(
        kern,
        grid=(B, nt),
        in_specs=[
            pl.BlockSpec((1, TB, PROJ_MAIN_A), lambda b, t: (b, t, 0)),
            pl.BlockSpec((1, TB, LANES), lambda b, t: (b, t, 0)),
            pl.BlockSpec((TB, LANES), lambda b, t: (t, 0)),
            pl.BlockSpec((TB, LANES), lambda b, t: (t, 0)),
            full((1, LANES)), full((CONV_W, 2 * D_QKM)), full((1, 2 * D_QKM)),
            full((1, D_M)), full((1, D_R)), full((H_R, L, L)), full((L, LANES)), full((1, LANES)),
            per_b((H_M, DK_M, DV_M)), per_b((H_M, DK_M)), per_b((H_M, LANES)),
            per_b((CONV_W - 1, 2 * D_QKM)), per_b((H_R, DK_R, DV_R)),
        ],
        out_specs=[
            pl.BlockSpec((1, TB, D_M + D_R), lambda b, t: (b, t, 0)),
            per_b((H_M, DK_M, DV_M)), per_b((H_M, DK_M)), per_b((H_M, LANES)),
            per_b((CONV_W - 1, 2 * D_QKM)), per_b((H_R, DK_R, DV_R)),
        ],
        out_shape=[
            jax.ShapeDtypeStruct((B, T, D_M + D_R), BF16),
            jax.ShapeDtypeStruct((B, H_M, DK_M, DV_M), F32),
            jax.ShapeDtypeStruct((B, H_M, DK_M), F32),
            jax.ShapeDtypeStruct((B, H_M, LANES), F32),
            jax.ShapeDtypeStruct((B, CONV_W - 1, 2 * D_QKM), F32),
            jax.ShapeDtypeStruct((B, H_R, DK_R, DV_R), F32),
        ],
        scratch_shapes=[
            pltpu.VMEM((tb8 + 8, 2 * D_QKM), F32),
            pltpu.VMEM((tb8, 2 * D_QKM), F32),
            pltpu.VMEM((tb8, 2 * D_QKR), F32),
        ],
        compiler_params=_params(("parallel", "arbitrary")),
        name="even_mixer",
    )(p, gates, cos_r, sin_r, gbias, cw, cb, gnm, gnr, decay, rtab, cdec, C0, n0, m0, cv0, S0)


def _out_proj_kernel(h_ref, w_ref, x_ref, g_ref, y_ref):
    o = _dot(h_ref[...], w_ref[...])
    y_ref[...] = x_ref[...] + _rms(o, g_ref[...])


def _out_proj(h, w, x, g):
    M, K = h.shape
    N = w.shape[1]
    tm = min(M, 512)
    return pl.pallas_call(
        _out_proj_kernel,
        grid=(M // tm,),
        in_specs=[
            pl.BlockSpec((tm, K), lambda i: (i, 0)),
            pl.BlockSpec((K, N), lambda i: (0, 0)),
            pl.BlockSpec((tm, N), lambda i: (i, 0)),
            pl.BlockSpec((1, N), lambda i: (0, 0)),
        ],
        out_specs=pl.BlockSpec((tm, N), lambda i: (i, 0)),
        out_shape=jax.ShapeDtypeStruct((M, N), F32),
        compiler_params=_params(("parallel",)),
        name="out_proj",
    )(h, w, x, g)


def _in_proj_c_kernel(x_ref, g_ref, w_ref, ra_ref, rb_ref, rc_ref,
                      q_ref, kf_ref, kb_ref, vf_ref, vb_ref, z_ref):
    xn = _rms(x_ref[...], g_ref[...]).astype(BF16)
    ra = ra_ref[...]
    rb = rb_ref[...]
    rc = rc_ref[...]
    half = ROPE_DIM // 2

    def rope(y):
        outs = []
        for gi in range(y.shape[1] // LANES):
            yg = y[:, gi * LANES:(gi + 1) * LANES]
            outs.append(yg * ra + pltpu.roll(yg, half, 1) * rb + pltpu.roll(yg, LANES - half, 1) * rc)
        return jnp.concatenate(outs, axis=1)

    q = rope(_dot(xn, w_ref[:, 0:D_QKC]))
    q_ref[...] = (q * (HD_C ** -0.5)).astype(BF16)
    k = rope(_dot(xn, w_ref[:, D_QKC:2 * D_QKC]))
    kf_ref[...] = k
    kb_ref[...] = k.astype(BF16)
    v = _dot(xn, w_ref[:, 2 * D_QKC:2 * D_QKC + D_C])
    vf_ref[...] = v
    vb_ref[...] = v.astype(BF16)
    z_ref[...] = _dot(xn, w_ref[:, 2 * D_QKC + D_C:]).astype(BF16)


def _in_proj_c(x, g, w, ra, rb, rc, tm):
    M, K = x.shape
    n_rt = ra.shape[0] // tm
    row = lambda n: pl.BlockSpec((tm, n), lambda i: (i, 0))
    tab = pl.BlockSpec((tm, LANES), lambda i: (i % n_rt, 0))
    return pl.pallas_call(
        _in_proj_c_kernel,
        grid=(M // tm,),
        in_specs=[
            row(K),
            pl.BlockSpec((1, K), lambda i: (0, 0)),
            pl.BlockSpec(w.shape, lambda i: (0, 0)),
            tab, tab, tab,
        ],
        out_specs=[row(D_QKC), row(D_QKC), row(D_QKC), row(D_C), row(D_C), row(D_C)],
        out_shape=[
            jax.ShapeDtypeStruct((M, D_QKC), BF16),
            jax.ShapeDtypeStruct((M, D_QKC), F32),
            jax.ShapeDtypeStruct((M, D_QKC), BF16),
            jax.ShapeDtypeStruct((M, D_C), F32),
            jax.ShapeDtypeStruct((M, D_C), BF16),
            jax.ShapeDtypeStruct((M, D_C), BF16),
        ],
        compiler_params=_params(("parallel",)),
        name="in_proj_odd",
    )(x, g, w, ra, rb, rc)


def _lambda_value(lq_ref, lam_init):
    lf = lq_ref[...]
    l1 = jnp.sum(lf[0:1] * lf[1:2], axis=1, keepdims=True)
    l2 = jnp.sum(lf[2:3] * lf[3:4], axis=1, keepdims=True)
    return jnp.exp(l1) - jnp.exp(l2) + lam_init


def _diff_attn_kernel(lq_ref, q_ref, k_ref, v_ref, z_ref, gn_ref, o_ref, *, tq, lam_init):
    i = pl.program_id(2)
    q = q_ref[0]
    lane = lax.broadcasted_iota(jnp.int32, (tq, LANES), 1)
    zero = jnp.zeros_like(q)
    q2 = jnp.concatenate([jnp.where(lane < HD_C, q, zero), jnp.where(lane >= HD_C, q, zero)], axis=0)

    def step(j, carry, masked):
        m, l, acc = carry
        rows = pl.ds(pl.multiple_of(j * tq, tq), tq)
        kb = k_ref[0, rows, :]
        vb = v_ref[0, rows, :]
        s = _dot_nt(q2, kb)
        if masked:
            r = lax.broadcasted_iota(jnp.int32, (2 * tq, tq), 0)
            c = lax.broadcasted_iota(jnp.int32, (2 * tq, tq), 1)
            r = jnp.where(r >= tq, r - tq, r)
            s = jnp.where(c <= r, s, -jnp.inf)
        m_new = jnp.maximum(m, jnp.max(s, axis=1, keepdims=True))
        alpha = jnp.exp(m - m_new)
        p = jnp.exp(s - m_new)
        l = alpha * l + jnp.sum(p, axis=1, keepdims=True)
        acc = alpha * acc + _dot(p.astype(BF16), vb)
        return m_new, l, acc

    init = (jnp.full((2 * tq, 1), -jnp.inf, F32), jnp.zeros((2 * tq, 1), F32),
            jnp.zeros((2 * tq, LANES), F32))
    carry = lax.fori_loop(0, i, lambda j, c: step(j, c, False), init)
    m, l, acc = step(i, carry, True)
    out = acc / l
    lam = _lambda_value(lq_ref, lam_init)
    a = out[0:tq] - lam * out[tq:2 * tq]
    y = _rms(a, gn_ref[0]) * (1.0 - lam_init)
    z = z_ref[0].astype(F32)
    o_ref[0] = (y * _silu(z)).astype(o_ref.dtype)


def _diff_attn_prompt(lq, q, k, v, z, gn, lam_init, tq):
    B, T, _ = q.shape
    kern = functools.partial(_diff_attn_kernel, tq=tq, lam_init=lam_init)
    blk = pl.BlockSpec((1, tq, LANES), lambda b, h, i: (b, i, h))
    kv = pl.BlockSpec((1, T, LANES), lambda b, h, i: (b, 0, h))
    return pl.pallas_call(
        kern,
        grid=(B, H_C, T // tq),
        in_specs=[
            pl.BlockSpec(lq.shape, lambda b, h, i: (0, 0)),
            blk, kv, kv, blk,
            pl.BlockSpec((1, 1, LANES), lambda b, h, i: (h, 0, 0)),
        ],
        out_specs=blk,
        out_shape=jax.ShapeDtypeStruct((B, T, D_C), BF16),
        compiler_params=_params(("parallel", "parallel", "arbitrary")),
        name="diff_attn_prompt",
    )(lq, q, k, v, z, gn)


def _paged_attn_kernel(pt_ref, lq_ref, qr_ref, *rest, P, lam_init, Tn):
    k_refs = rest[:P]
    v_refs = rest[P:2 * P]
    kn_ref, vn_ref, bias_ref, z_ref, gn_ref, o_ref, m_sc, l_sc, acc_sc = rest[2 * P:]
    j = pl.program_id(1)
    GR = 2 * Tn

    @pl.when(j == 0)
    def _():
        m_sc[...] = jnp.full(m_sc.shape, -jnp.inf, F32)
        l_sc[...] = jnp.zeros(l_sc.shape, F32)
        acc_sc[...] = jnp.zeros(acc_sc.shape, F32)

    qr = qr_ref[0]

    def page(K, V, bias):
        s = _dot_nt(qr, K.astype(BF16))
        if bias is not None:
            s = s + bias
        m_prev = m_sc[...]
        m_new = jnp.maximum(m_prev, jnp.max(s, axis=1, keepdims=True))
        alpha = jnp.exp(m_prev - m_new)
        p = jnp.exp(s - m_new)
        l_sc[...] = alpha * l_sc[...] + jnp.sum(p, axis=1, keepdims=True)
        m_sc[...] = m_new
        Vb = V.astype(BF16)
        for hp in range(H_C):
            rs = slice(hp * GR, (hp + 1) * GR)
            pv = _dot(p[rs, :].astype(BF16), Vb[:, hp * LANES:(hp + 1) * LANES])
            acc_sc[rs, :] = alpha[rs, :] * acc_sc[rs, :] + pv

    for pi in range(P):
        page(k_refs[pi][0], v_refs[pi][0], None)

    @pl.when(j == pl.num_programs(1) - 1)
    def _():
        page(kn_ref[0], vn_ref[0], bias_ref[...])
        out = acc_sc[...] / l_sc[...]
        lam = _lambda_value(lq_ref, lam_init)
        for hp in range(H_C):
            a = out[hp * GR:hp * GR + Tn] - lam * out[hp * GR + Tn:(hp + 1) * GR]
            y = _rms(a, gn_ref[hp:hp + 1, :]) * (1.0 - lam_init)
            z = z_ref[0, :, hp * LANES:(hp + 1) * LANES].astype(F32)
            o_ref[0, :, hp * LANES:(hp + 1) * LANES] = (y * _silu(z)).astype(o_ref.dtype)


def _paged_attn(page_table, lq, qrows, ck, cv, kn, vn, bias, z, gn, lam_init, P):
    B, R, _ = qrows.shape
    n_pages = page_table.shape[1]
    Tn = z.shape[1]
    page_size = ck.shape[1]
    kern = functools.partial(_paged_attn_kernel, P=P, lam_init=lam_init, Tn=Tn)

    def page_spec(pi):
        return pl.BlockSpec((1, page_size, ck.shape[2]), lambda b, j, pt: (pt[b, j * P + pi], 0, 0))

    per_b = lambda shape: pl.BlockSpec((1,) + shape, lambda b, j, pt: (b,) + (0,) * len(shape))
    full = lambda shape: pl.BlockSpec(shape, lambda b, j, pt: (0,) * len(shape))
    grid_spec = pltpu.PrefetchScalarGridSpec(
        num_scalar_prefetch=1,
        grid=(B, n_pages // P),
        in_specs=[full(lq.shape), per_b(qrows.shape[1:])]
        + [page_spec(pi) for pi in range(P)] + [page_spec(pi) for pi in range(P)]
        + [per_b(kn.shape[1:]), per_b(vn.shape[1:]), full(bias.shape), per_b(z.shape[1:]), full(gn.shape)],
        out_specs=per_b((Tn, D_C)),
        scratch_shapes=[pltpu.VMEM((R, 1), F32), pltpu.VMEM((R, 1), F32), pltpu.VMEM((R, LANES), F32)],
    )
    return pl.pallas_call(
        kern,
        grid_spec=grid_spec,
        out_shape=jax.ShapeDtypeStruct((B, Tn, D_C), BF16),
        compiler_params=_params(("parallel", "arbitrary")),
        name="paged_diff_attn",
    )(page_table, lq, qrows, *([ck] * P), *([cv] * P), kn, vn, bias, z, gn)


def _rope_angles(pos, rot_dim, theta):
    half = rot_dim // 2
    inv = jnp.power(theta, -jnp.arange(half, dtype=F32) / half)
    return pos.astype(F32)[:, None] * inv[None, :]


def _retention_tables(L):
    lg = jnp.log(1.0 - jnp.power(2.0, -5.0 - jnp.arange(H_R, dtype=F32)))
    j = jnp.arange(L, dtype=F32)
    rel = j[:, None] - j[None, :]
    decay = jnp.where((rel >= 0)[None], jnp.exp(rel[None] * lg[:, None, None]), 0.0)
    q_in = jnp.exp((j + 1.0)[:, None] * lg)
    k_out = jnp.exp((L - 1.0 - j)[:, None] * lg)
    rtab = jnp.zeros((L, LANES), F32).at[:, 0:H_R].set(q_in).at[:, H_R:2 * H_R].set(k_out)
    cdec = jnp.zeros((1, LANES), F32).at[0, 0:H_R].set(jnp.exp(L * lg))
    return decay, rtab, cdec


def _partial_rope_tables(pos):
    ang = _rope_angles(pos, ROPE_DIM, ROPE_THETA)
    half = ROPE_DIM // 2
    cos, sin = jnp.cos(ang), jnp.sin(ang)
    T = pos.shape[0]
    ra = jnp.ones((T, HD_C), F32).at[:, 0:half].set(cos).at[:, half:ROPE_DIM].set(cos)
    rb = jnp.zeros((T, HD_C), F32).at[:, half:ROPE_DIM].set(sin)
    rc = jnp.zeros((T, HD_C), F32).at[:, 0:half].set(-sin)
    rep = LANES // HD_C
    return jnp.tile(ra, (1, rep)), jnp.tile(rb, (1, rep)), jnp.tile(rc, (1, rep))


def _forward(x, pos0, st_C, st_n, st_m, st_conv, st_S, attend, W):
    B, T, _ = x.shape
    M = B * T
    act_dtype = BF16 if T >= CHUNK else F32
    pos = pos0 + jnp.arange(T)

    x2 = x.reshape(M, D_MODEL)
    p, gates = _in_proj_a(x2, W["norm_pre"][0:1], W["w_main_a"], W["w_gate_a"], act_dtype)
    L = CHUNK if T % CHUNK == 0 else T
    TB = min(T, 256)
    ang = _rope_angles(pos, DK_R, RET_THETA)
    decay, rtab, cdec = _retention_tables(L)
    consts = (W["gbias"], W["conv_w"], W["conv_b"], W["gn_m"], W["gn_r"], decay, rtab, cdec)
    m0 = jnp.broadcast_to(st_m[..., None], st_m.shape + (LANES,))
    hcat, C, n, m, conv, S = _even_mixer(
        p.reshape(B, T, PROJ_MAIN_A), gates.reshape(B, T, LANES), jnp.cos(ang), jnp.sin(ang),
        consts, st_C, st_n, m0, st_conv, st_S, TB, L)
    x2 = _out_proj(hcat.reshape(M, D_M + D_R), W["w_out_a"], x2, W["norm_post"][0:1])

    tm = min(M, 512)
    n_rep = max(1, tm // T)
    ra, rb, rc = _partial_rope_tables(jnp.tile(pos, n_rep))
    q, kf, kb, vf, vb, z = _in_proj_c(x2, W["norm_pre"][1:2], W["w_in_c"], ra, rb, rc, tm)
    a = attend(q.reshape(B, T, D_QKC), kf.reshape(B, T, D_QKC), kb.reshape(B, T, D_QKC),
               vf.reshape(B, T, D_C), vb.reshape(B, T, D_C), z.reshape(B, T, D_C))
    y = _out_proj(a.reshape(M, D_C), W["w_out_c"], x2, W["norm_post"][1:2])
    return (y.reshape(B, T, D_MODEL), C[None], n[None], m[None, :, :, 0], conv[None], S[None],
            kf.reshape(1, B, T, 2 * H_C, HD_C), vf.reshape(1, B, T, H_C, 2 * HD_C))


def kernel(x_prompt, x_sample, state_mlstm_C, state_mlstm_n, state_mlstm_m, state_mlstm_conv, state_ret_S, cache_k, cache_v, page_table, norm_pre, norm_post, w_in_a, b_gate_i, b_gate_f, conv_w, conv_b, gn_mlstm, gn_ret, w_out_a, w_in_c, lambda_qk, gn_diff, w_out_c):
    g0 = 2 * D_QKM + 2 * D_M
    wa = w_in_a[0]
    W = {
        "norm_pre": norm_pre, "norm_post": norm_post,
        "w_main_a": jnp.concatenate([wa[:, :g0], wa[:, g0 + 2 * H_M:]], axis=1).astype(BF16),
        "w_gate_a": jnp.pad(wa[:, g0:g0 + 2 * H_M], ((0, 0), (0, LANES - 2 * H_M))).astype(BF16),
        "gbias": jnp.pad(jnp.concatenate([b_gate_i[0], b_gate_f[0]])[None], ((0, 0), (0, LANES - 2 * H_M))),
        "conv_w": conv_w[0], "conv_b": conv_b[0][None],
        "gn_m": gn_mlstm[0].reshape(1, D_M), "gn_r": gn_ret[0].reshape(1, D_R),
        "w_out_a": w_out_a[0].astype(BF16),
        "w_in_c": w_in_c[0].astype(BF16),
        "w_out_c": w_out_c[0].astype(BF16),
    }
    lam_init = 0.8 - 0.6 * math.exp(-0.3 * 1)
    lq = lambda_qk[0]
    gn_d = gn_diff[0]

    Bp, Tp, _ = x_prompt.shape

    def attend_prompt(q, kf, kb, vf, vb, z):
        return _diff_attn_prompt(lq, q, kb, vb, z, gn_d[:, None, :], lam_init, tq=256)

    outs_p = _forward(
        x_prompt, 0,
        jnp.zeros((Bp, H_M, DK_M, DV_M), F32), jnp.zeros((Bp, H_M, DK_M), F32),
        jnp.zeros((Bp, H_M), F32), jnp.zeros((Bp, CONV_W - 1, 2 * D_QKM), F32),
        jnp.zeros((Bp, H_R, DK_R, DV_R), F32), attend_prompt, W)

    Bs, Ts, _ = x_sample.shape
    n_pool, page_size = cache_k.shape[1], cache_k.shape[2]
    past = page_table.shape[1] * page_size
    ck = cache_k[0].reshape(n_pool, page_size, D_QKC)
    cv = cache_v[0].reshape(n_pool, page_size, D_C)

    def attend_sample(q, kf, kb, vf, vb, z):
        head_of_lane = jnp.arange(D_QKC) // HD_C
        sel = (head_of_lane[None, :] == jnp.arange(2 * H_C)[:, None])
        qrows = jnp.where(sel[None, :, None, :], q[:, None, :, :], jnp.zeros((), q.dtype))
        qrows = qrows.reshape(Bs, 2 * H_C * Ts, D_QKC)
        kn = jnp.pad(kf, ((0, 0), (0, page_size - Ts), (0, 0)))
        vn = jnp.pad(vf, ((0, 0), (0, page_size - Ts), (0, 0)))
        tok = jnp.arange(2 * H_C * Ts) % Ts
        col = jnp.arange(page_size)
        bias = jnp.where(col[None, :] <= tok[:, None], 0.0, -jnp.inf).astype(F32)
        return _paged_attn(page_table, lq, qrows, ck, cv, kn, vn, bias, z, gn_d, lam_init, P=4)

    outs_s = _forward(x_sample, past, state_mlstm_C[0], state_mlstm_n[0], state_mlstm_m[0],
                      state_mlstm_conv[0], state_ret_S[0], attend_sample, W)

    return (outs_p[0], outs_s[0]) + tuple(outs_p[1:]) + tuple(outs_s[1:])
```

```python
import functools
import math

import jax
import jax.numpy as jnp
from jax import lax
from jax.experimental import pallas as pl
from jax.experimental.pallas import tpu as pltpu

F32 = jnp.float32
BF16 = jnp.bfloat16

D_MODEL = 1024
H_M, DK_M, DV_M = 4, 256, 256
D_QKM = H_M * DK_M
D_M = H_M * DV_M
CONV_W = 4
H_R, DK_R, DV_R = 4, 256, 256
D_QKR = H_R * DK_R
D_R = H_R * DV_R
RET_THETA = 10000.0
H_C, HD_C = 8, 64
D_QKC = 2 * H_C * HD_C
D_C = H_C * 2 * HD_C
ROPE_THETA = 500000.0
ROPE_DIM = HD_C // 4
CHUNK = 256
EPS = 1e-6
PROJ_MAIN_A = 2 * D_QKM + 2 * D_M + 2 * D_QKR + 2 * D_R
LANES = 128
ONES_ROWS = 16
LOG2E = math.log2(math.e)
VMEM_LIMIT = 52 * 1024 * 1024

OFF_QKM, OFF_VM, OFF_ZM = 0, 2 * D_QKM, 2 * D_QKM + D_M
OFF_QR = 2 * D_QKM + 2 * D_M
OFF_KR, OFF_VR, OFF_ZR = OFF_QR + D_QKR, OFF_QR + 2 * D_QKR, OFF_QR + 2 * D_QKR + D_R


def _params(sem):
    return pltpu.CompilerParams(dimension_semantics=sem, vmem_limit_bytes=VMEM_LIMIT)


def _rms(x, g):
    return x * lax.rsqrt(jnp.mean(x * x, axis=-1, keepdims=True) + EPS) * g


def _silu(x):
    return x * jax.nn.sigmoid(x)


def _dot(a, b):
    return jnp.dot(a, b, preferred_element_type=F32)


def _dot_nt(a, b):
    return lax.dot_general(a, b, (((1,), (1,)), ((), ())), preferred_element_type=F32)


def _dot_tn(a, b):
    return lax.dot_general(a, b, (((0,), (0,)), ((), ())), preferred_element_type=F32)


def _in_proj_a_kernel(x_ref, g_ref, w_ref, wg_ref, o_ref, og_ref, xn_ref):
    @pl.when(pl.program_id(1) == 0)
    def _():
        xn = _rms(x_ref[...], g_ref[...]).astype(BF16)
        xn_ref[...] = xn
        og_ref[...] = _dot(xn, wg_ref[...])

    o_ref[...] = _dot(xn_ref[...], w_ref[...]).astype(o_ref.dtype)


def _in_proj_a(x, g, w_main, w_gate, out_dtype):
    M, K = x.shape
    N = w_main.shape[1]
    tm = min(M, 1024)
    tn = 1024
    return pl.pallas_call(
        _in_proj_a_kernel,
        grid=(M // tm, N // tn),
        in_specs=[
            pl.BlockSpec((tm, K), lambda i, j: (i, 0)),
            pl.BlockSpec((1, K), lambda i, j: (0, 0)),
            pl.BlockSpec((K, tn), lambda i, j: (0, j)),
            pl.BlockSpec((K, LANES), lambda i, j: (0, 0)),
        ],
        out_specs=[
            pl.BlockSpec((tm, tn), lambda i, j: (i, j)),
            pl.BlockSpec((tm, LANES), lambda i, j: (i, 0)),
        ],
        out_shape=[
            jax.ShapeDtypeStruct((M, N), out_dtype),
            jax.ShapeDtypeStruct((M, LANES), F32),
        ],
        scratch_shapes=[pltpu.VMEM((tm, K), BF16)],
        compiler_params=_params(("parallel", "arbitrary")),
        name="in_proj_even",
    )(x, g, w_main, w_gate)


def _even_mixer_kernel(p_ref, g_ref, cos_ref, sin_ref, gb_ref, cw_ref, cb_ref, gnm_ref, gnr_ref,
                       dec_ref, rtab_ref, cdec_ref, C0_ref, n0_ref, m0_ref, cv0_ref, S0_ref,
                       h_ref, C_ref, n_ref, m_ref, cv_ref, S_ref, xbuf, qkc, qkr, *, TB, L):
    t = pl.program_id(1)
    HIST = 8

    @pl.when(t == 0)
    def _():
        C_ref[...] = C0_ref[...]
        n_ref[...] = n0_ref[...]
        m_ref[...] = m0_ref[...]
        S_ref[...] = S0_ref[...]
        xbuf[HIST - (CONV_W - 1):HIST, :] = cv0_ref[0]

    @pl.when(t > 0)
    def _():
        xbuf[0:HIST, :] = xbuf[TB:TB + HIST, :]

    for cg in range(2 * D_QKM // 256):
        cs = slice(cg * 256, (cg + 1) * 256)
        x = p_ref[0, :, cs].astype(F32)
        xbuf[HIST:HIST + TB, cs] = x
        acc = cb_ref[:, cs] + xbuf[HIST - 3:HIST - 3 + TB, cs] * cw_ref[0:1, cs]
        acc = acc + xbuf[HIST - 2:HIST - 2 + TB, cs] * cw_ref[1:2, cs]
        acc = acc + xbuf[HIST - 1:HIST - 1 + TB, cs] * cw_ref[2:3, cs]
        acc = acc + x * cw_ref[3:4, cs]
        y = _silu(acc)
        if cg < D_QKM // 256:
            y = y * (DK_M ** -0.5)
        qkc[0:TB, cs] = y
    cv_ref[0] = xbuf[HIST + TB - (CONV_W - 1):HIST + TB, :]

    cos = cos_ref[...]
    sin = sin_ref[...]
    half = DK_R // 2
    for which, off, scale in ((0, OFF_QR, 1.0), (1, OFF_KR, DK_R ** -0.5)):
        for h in range(H_R):
            x1 = p_ref[0, :, off + h * DK_R:off + h * DK_R + half].astype(F32)
            x2 = p_ref[0, :, off + h * DK_R + half:off + (h + 1) * DK_R].astype(F32)
            base = which * D_QKR + h * DK_R
            qkr[0:TB, base:base + half] = (x1 * cos - x2 * sin) * scale
            qkr[0:TB, base + half:base + DK_R] = (x1 * sin + x2 * cos) * scale

    def chunk(rows):
        gates = g_ref[0, rows, :] + gb_ref[...]
        ti = lax.broadcasted_iota(jnp.int32, (L, L), 0)
        si = lax.broadcasted_iota(jnp.int32, (L, L), 1)
        eye = ti == si
        low = si <= ti
        upp = ti <= si
        for h in range(H_M):
            ig = gates[:, h:h + 1]
            lf = jax.nn.log_sigmoid(gates[:, H_M + h:H_M + h + 1])
            lf_r = jnp.sum(jnp.where(eye, lf, 0.0), axis=0, keepdims=True)
            i_r = jnp.sum(jnp.where(eye, ig, 0.0), axis=0, keepdims=True)
            b_c = jnp.sum(jnp.where(low, lf_r, 0.0), axis=1, keepdims=True)
            b_r = jnp.sum(jnp.where(upp, lf, 0.0), axis=0, keepdims=True)
            m_prev = m_ref[0, h:h + 1, 0:1]
            inter = b_c + m_prev
            Dm = jnp.where(low, b_c - b_r + i_r, -jnp.inf)
            m_t = jnp.maximum(inter, jnp.max(Dm, axis=1, keepdims=True))
            w = jnp.exp(Dm - m_t)
            s_int = jnp.exp(inter - m_t)
            q = qkc[rows, h * DK_M:(h + 1) * DK_M]
            k = qkc[rows, D_QKM + h * DK_M:D_QKM + (h + 1) * DK_M]
            vb = p_ref[0, rows, OFF_VM + h * DV_M:OFF_VM + (h + 1) * DV_M].astype(BF16)
            qb = q.astype(BF16)
            qk = _dot_nt(qb, k.astype(BF16)) * w
            C = C_ref[0, h]
            nvec = n_ref[0, h:h + 1, :]
            num = _dot(qk.astype(BF16), vb) + s_int * _dot(qb, C.astype(BF16))
            den = (jnp.sum(qk, axis=1, keepdims=True)
                   + s_int * jnp.sum(q * nvec, axis=1, keepdims=True))
            hh = num / jnp.maximum(jnp.abs(den), jnp.exp(-m_t))
            y = _rms(hh, gnm_ref[:, h * DV_M:(h + 1) * DV_M])
            z = p_ref[0, rows, OFF_ZM + h * DV_M:OFF_ZM + (h + 1) * DV_M].astype(F32)
            h_ref[0, rows, h * DV_M:(h + 1) * DV_M] = (y * _silu(z)).astype(h_ref.dtype)
            bL = b_c[L - 1:L, :]
            dec = bL - b_c + ig
            m_new = jnp.maximum(bL + m_prev, jnp.max(dec, axis=0, keepdims=True))
            ws = jnp.exp(dec - m_new)
            sc = jnp.exp(bL + m_prev - m_new)
            kw = k * ws
            C_ref[0, h] = sc * C + _dot_tn(kw.astype(BF16), vb)
            n_ref[0, h:h + 1, :] = sc * nvec + jnp.sum(kw, axis=0, keepdims=True)
            m_ref[0, h:h + 1, :] = jnp.broadcast_to(m_new, (1, LANES))
        for h in range(H_R):
            q = qkr[rows, h * DK_R:(h + 1) * DK_R]
            k = qkr[rows, D_QKR + h * DK_R:D_QKR + (h + 1) * DK_R]
            vb = p_ref[0, rows, OFF_VR + h * DV_R:OFF_VR + (h + 1) * DV_R].astype(BF16)
            qk = _dot_nt(q.astype(BF16), k.astype(BF16)) * dec_ref[h]
            S = S_ref[0, h]
            q_in = rtab_ref[:, h:h + 1]
            k_out = rtab_ref[:, H_R + h:H_R + h + 1]
            o = _dot(qk.astype(BF16), vb) + _dot((q * q_in).astype(BF16), S.astype(BF16))
            S_ref[0, h] = cdec_ref[:, h:h + 1] * S + _dot_tn((k * k_out).astype(BF16), vb)
            y = _rms(o, gnr_ref[:, h * DV_R:(h + 1) * DV_R])
            z = p_ref[0, rows, OFF_ZR + h * DV_R:OFF_ZR + (h + 1) * DV_R].astype(F32)
            h_ref[0, rows, D_M + h * DV_R:D_M + (h + 1) * DV_R] = (y * _silu(z)).astype(h_ref.dtype)

    n_chunks = TB // L
    if n_chunks == 1:
        chunk(slice(0, L))
    else:
        def body(c, carry):
            chunk(pl.ds(pl.multiple_of(c * L, L), L))
            return carry
        lax.fori_loop(0, n_chunks, body, 0)


def _even_mixer(p, gates, cos_r, sin_r, consts, C0, n0, m0, cv0, S0, TB, L):
    B, T, _ = p.shape
    gbias, cw, cb, gnm, gnr, decay, rtab, cdec = consts
    nt = T // TB
    tb8 = -(-TB // 8) * 8
    full = lambda shape: pl.BlockSpec(shape, lambda b, t: (0,) * len(shape))
    per_b = lambda shape: pl.BlockSpec((1,) + shape, lambda b, t: (b,) + (0,) * len(shape))
    kern = functools.partial(_even_mixer_kernel, TB=TB, L=L)
    return pl.pallas_call(
        kern,
        grid=(B, nt),
        in_specs=[
            pl.BlockSpec((1, TB, PROJ_MAIN_A), lambda b, t: (b, t, 0)),
            pl.BlockSpec((1, TB, LANES), lambda b, t: (b, t, 0)),
            pl.BlockSpec((TB, LANES), lambda b, t: (t, 0)),
            pl.BlockSpec((TB, LANES), lambda b, t: (t, 0)),
            full((1, LANES)), full((CONV_W, 2 * D_QKM)), full((1, 2 * D_QKM)),
            full((1, D_M)), full((1, D_R)), full((H_R, L, L)), full((L, LANES)), full((1, LANES)),
            per_b((H_M, DK_M, DV_M)), per_b((H_M, DK_M)), per_b((H_M, LANES)),
            per_b((CONV_W - 1, 2 * D_QKM)), per_b((H_R, DK_R, DV_R)),
        ],
        out_specs=[
            pl.BlockSpec((1, TB, D_M + D_R), lambda b, t: (b, t, 0)),
            per_b((H_M, DK_M, DV_M)), per_b((H_M, DK_M)), per_b((H_M, LANES)),
            per_b((CONV_W - 1, 2 * D_QKM)), per_b((H_R, DK_R, DV_R)),
        ],
        out_shape=[
            jax.ShapeDtypeStruct((B, T, D_M + D_R), BF16),
            jax.ShapeDtypeStruct((B, H_M, DK_M, DV_M), F32),
            jax.ShapeDtypeStruct((B, H_M, DK_M), F32),
            jax.ShapeDtypeStruct((B, H_M, LANES), F32),
            jax.ShapeDtypeStruct((B, CONV_W - 1, 2 * D_QKM), F32),
            jax.ShapeDtypeStruct((B, H_R, DK_R, DV_R), F32),
        ],
        scratch_shapes=[
            pltpu.VMEM((tb8 + 8, 2 * D_QKM), F32),
            pltpu.VMEM((tb8, 2 * D_QKM), F32),
            pltpu.VMEM((tb8, 2 * D_QKR), F32),
        ],
        compiler_params=_params(("parallel", "arbitrary")),
        name="even_mixer",
    )(p, gates, cos_r, sin_r, gbias, cw, cb, gnm, gnr, decay, rtab, cdec, C0, n0, m0, cv0, S0)


def _out_proj_kernel(h_ref, w_ref, x_ref, g_ref, y_ref):
    o = _dot(h_ref[...], w_ref[...])
    y_ref[...] = x_ref[...] + _rms(o, g_ref[...])


def _out_proj(h, w, x, g):
    M, K = h.shape
    N = w.shape[1]
    tm = min(M, 512)
    return pl.pallas_call(
        _out_proj_kernel,
        grid=(M // tm,),
        in_specs=[
            pl.BlockSpec((tm, K), lambda i: (i, 0)),
            pl.BlockSpec((K, N), lambda i: (0, 0)),
            pl.BlockSpec((tm, N), lambda i: (i, 0)),
            pl.BlockSpec((1, N), lambda i: (0, 0)),
        ],
        out_specs=pl.BlockSpec((tm, N), lambda i: (i, 0)),
        out_shape=jax.ShapeDtypeStruct((M, N), F32),
        compiler_params=_params(("parallel",)),
        name="out_proj",
    )(h, w, x, g)


def _in_proj_c_kernel(x_ref, g_ref, w_ref, ra_ref, rb_ref, rc_ref,
                      q_ref, kf_ref, kb_ref, vf_ref, vb_ref, z_ref):
    xn = _rms(x_ref[...], g_ref[...]).astype(BF16)
    ra = ra_ref[...]
    rb = rb_ref[...]
    rc = rc_ref[...]
    half = ROPE_DIM // 2

    def rope(y):
        outs = []
        for gi in range(y.shape[1] // LANES):
            yg = y[:, gi * LANES:(gi + 1) * LANES]
            outs.append(yg * ra + pltpu.roll(yg, half, 1) * rb + pltpu.roll(yg, LANES - half, 1) * rc)
        return jnp.concatenate(outs, axis=1)

    q = rope(_dot(xn, w_ref[:, 0:D_QKC]))
    q_ref[...] = (q * (HD_C ** -0.5 * LOG2E)).astype(BF16)
    k = rope(_dot(xn, w_ref[:, D_QKC:2 * D_QKC]))
    kf_ref[...] = k
    kb_ref[...] = k.astype(BF16)
    v = _dot(xn, w_ref[:, 2 * D_QKC:2 * D_QKC + D_C])
    vf_ref[...] = v
    vb_ref[...] = v.astype(BF16)
    z_ref[...] = _dot(xn, w_ref[:, 2 * D_QKC + D_C:]).astype(BF16)


def _in_proj_c(x, g, w, ra, rb, rc, tm):
    M, K = x.shape
    n_rt = ra.shape[0] // tm
    row = lambda n: pl.BlockSpec((tm, n), lambda i: (i, 0))
    tab = pl.BlockSpec((tm, LANES), lambda i: (i % n_rt, 0))
    return pl.pallas_call(
        _in_proj_c_kernel,
        grid=(M // tm,),
        in_specs=[
            row(K),
            pl.BlockSpec((1, K), lambda i: (0, 0)),
            pl.BlockSpec(w.shape, lambda i: (0, 0)),
            tab, tab, tab,
        ],
        out_specs=[row(D_QKC), row(D_QKC), row(D_QKC), row(D_C), row(D_C), row(D_C)],
        out_shape=[
            jax.ShapeDtypeStruct((M, D_QKC), BF16),
            jax.ShapeDtypeStruct((M, D_QKC), F32),
            jax.ShapeDtypeStruct((M, D_QKC), BF16),
            jax.ShapeDtypeStruct((M, D_C), F32),
            jax.ShapeDtypeStruct((M, D_C), BF16),
            jax.ShapeDtypeStruct((M, D_C), BF16),
        ],
        compiler_params=_params(("parallel",)),
        name="in_proj_odd",
    )(x, g, w, ra, rb, rc)


def _lambda_value(lq_ref, lam_init):
    lf = lq_ref[...]
    l1 = jnp.sum(lf[0:1] * lf[1:2], axis=1, keepdims=True)
    l2 = jnp.sum(lf[2:3] * lf[3:4], axis=1, keepdims=True)
    return jnp.exp(l1) - jnp.exp(l2) + lam_init


def _diff_attn_kernel(lq_ref, q_ref, k_ref, v_ref, z_ref, gn_ref, o_ref, *, tq, lam_init):
    i = pl.program_id(2)
    q = q_ref[0]
    lane = lax.broadcasted_iota(jnp.int32, (tq, LANES), 1)
    zero = jnp.zeros_like(q)
    q2 = jnp.concatenate([jnp.where(lane < HD_C, q, zero), jnp.where(lane >= HD_C, q, zero)], axis=0)

    def step(j, carry, masked):
        m, l, acc = carry
        rows = pl.ds(pl.multiple_of(j * tq, tq), tq)
        kb = k_ref[0, rows, :]
        vb = v_ref[0, rows, :]
        s = _dot_nt(q2, kb)
        if masked:
            r = lax.broadcasted_iota(jnp.int32, (2 * tq, tq), 0)
            c = lax.broadcasted_iota(jnp.int32, (2 * tq, tq), 1)
            r = jnp.where(r >= tq, r - tq, r)
            s = jnp.where(c <= r, s, -jnp.inf)
        m_new = jnp.maximum(m, jnp.max(s, axis=1, keepdims=True))
        alpha = jnp.exp2(m - m_new)
        p = jnp.exp2(s - m_new)
        l = alpha * l + jnp.sum(p, axis=1, keepdims=True)
        acc = alpha * acc + _dot(p.astype(BF16), vb)
        return m_new, l, acc

    init = (jnp.full((2 * tq, 1), -jnp.inf, F32), jnp.zeros((2 * tq, 1), F32),
            jnp.zeros((2 * tq, LANES), F32))
    carry = lax.fori_loop(0, i, lambda j, c: step(j, c, False), init)
    m, l, acc = step(i, carry, True)
    out = acc / l
    lam = _lambda_value(lq_ref, lam_init)
    a = out[0:tq] - lam * out[tq:2 * tq]
    y = _rms(a, gn_ref[0]) * (1.0 - lam_init)
    z = z_ref[0].astype(F32)
    o_ref[0] = (y * _silu(z)).astype(o_ref.dtype)


def _diff_attn_prompt(lq, q, k, v, z, gn, lam_init, tq):
    B, T, _ = q.shape
    kern = functools.partial(_diff_attn_kernel, tq=tq, lam_init=lam_init)
    blk = pl.BlockSpec((1, tq, LANES), lambda b, h, i: (b, i, h))
    kv = pl.BlockSpec((1, T, LANES), lambda b, h, i: (b, 0, h))
    return pl.pallas_call(
        kern,
        grid=(B, H_C, T // tq),
        in_specs=[
            pl.BlockSpec(lq.shape, lambda b, h, i: (0, 0)),
            blk, kv, kv, blk,
            pl.BlockSpec((1, 1, LANES), lambda b, h, i: (h, 0, 0)),
        ],
        out_specs=blk,
        out_shape=jax.ShapeDtypeStruct((B, T, D_C), BF16),
        compiler_params=_params(("parallel", "parallel", "arbitrary")),
        name="diff_attn_prompt",
    )(lq, q, k, v, z, gn)


def _diff_attn_t_kernel(it_ref, jt_ref, lq_ref, q_ref, k_ref, v_ref, z_ref, gn_ref, bias_ref, o_ref,
                      vt_sc, q2_sc, s_sc, m_sc, acc_sc, p_sc, al_sc, *, tq, n_steps, lam_init, G):
    nb = v_ref.shape[1] // tq
    lane = lax.broadcasted_iota(jnp.int32, (tq, LANES), 1)

    def blk(idx):
        return pl.ds(pl.multiple_of(idx * tq, tq), tq)

    def head(g):
        return slice(g * LANES, (g + 1) * LANES)

    for g in range(G):
        for jb in range(nb):
            rows = slice(jb * tq, (jb + 1) * tq)
            vblk = v_ref[0, rows, head(g)].astype(F32)
            vt_sc[g, jb, 0:LANES, :] = vblk.T.astype(BF16)
            vt_sc[g, jb, LANES:LANES + ONES_ROWS, :] = jnp.ones((ONES_ROWS, tq), BF16)
            q = q_ref[0, rows, head(g)]
            zero = jnp.zeros_like(q)
            q2_sc[g, jb, 0:tq, :] = jnp.where(lane < HD_C, q, zero)
            q2_sc[g, jb, tq:2 * tq, :] = jnp.where(lane >= HD_C, q, zero)

    def scores(t, slot, first=False):
        for g in range(G):
            if first:
                s_sc[slot, g] = _dot_nt(k_ref[0, 0:tq, head(g)], q2_sc[g, 0])
            else:
                s_sc[slot, g] = _dot_nt(k_ref[0, blk(jt_ref[t]), head(g)], q2_sc[g, it_ref[t]])

    p_sc[...] = jnp.zeros(p_sc.shape, BF16)
    al_sc[...] = jnp.zeros(al_sc.shape, F32)

    def flush(g, j_prev):
        acc_sc[g] = al_sc[g] * acc_sc[g] + _dot(vt_sc[g, j_prev], p_sc[g])

    scores(0, 0, first=True)

    def step(t, slot):
        i = it_ref[t]
        j = jt_ref[t]
        j_prev = jt_ref[jnp.maximum(t - 1, 0)]

        @pl.when(j == 0)
        def _():
            m_sc[...] = jnp.full(m_sc.shape, -jnp.inf, F32)
            acc_sc[...] = jnp.zeros(acc_sc.shape, F32)

        scores(t + 1, 1 - slot)
        bias = bias_ref[(j == i).astype(jnp.int32)]
        for g in range(G):
            flush(g, j_prev)
            s = s_sc[slot, g] + bias
            m_prev = m_sc[g]
            m_new = jnp.maximum(m_prev, jnp.max(s, axis=0, keepdims=True))
            p_sc[g] = jnp.exp2(s - m_new).astype(BF16)
            al_sc[g] = jnp.exp2(m_prev - m_new)
            m_sc[g] = m_new

        @pl.when(j == i)
        def _():
            lam = _lambda_value(lq_ref, lam_init)
            for g in range(G):
                flush(g, j)
                p_sc[g] = jnp.zeros(p_sc.shape[1:], BF16)
                out_t = acc_sc[g, 0:LANES, :] / acc_sc[g, LANES:LANES + 1, :]
                a_t = out_t[:, 0:tq] - lam * out_t[:, tq:2 * tq]
                a_t = a_t * lax.rsqrt(jnp.mean(a_t * a_t, axis=0, keepdims=True) + EPS)
                y = a_t.T * gn_ref[g] * (1.0 - lam_init)
                z = z_ref[0, blk(i), head(g)].astype(F32)
                o_ref[0, blk(i), head(g)] = (y * _silu(z)).astype(o_ref.dtype)

    def body(u, carry):
        step(2 * u, 0)
        step(2 * u + 1, 1)
        return carry

    assert n_steps % 2 == 0
    lax.fori_loop(0, n_steps // 2, body, 0)


def _diff_attn_prompt_t(lq, q, k, v, z, gn, lam_init, tq, G):
    B, T, _ = q.shape
    nq = T // tq
    pairs = [(i, j) for i in range(nq) for j in range(i + 1)]
    n_steps = len(pairs)
    pairs.append((0, 0))
    i_tab = jnp.asarray([p[0] for p in pairs], jnp.int32)
    j_tab = jnp.asarray([p[1] for p in pairs], jnp.int32)
    qpos = jnp.arange(2 * tq) % tq
    causal = jnp.where(jnp.arange(tq)[:, None] <= qpos[None, :], 0.0, -jnp.inf).astype(F32)
    bias = jnp.stack([jnp.zeros_like(causal), causal])
    kern = functools.partial(_diff_attn_t_kernel, tq=tq, n_steps=n_steps, lam_init=lam_init, G=G)
    seq = pl.BlockSpec((1, T, G * LANES), lambda b, h, it, jt: (b, 0, h))
    grid_spec = pltpu.PrefetchScalarGridSpec(
        num_scalar_prefetch=2,
        grid=(B, H_C // G),
        in_specs=[
            pl.BlockSpec(lq.shape, lambda b, h, it, jt: (0, 0)),
            seq, seq, seq, seq,
            pl.BlockSpec((G, 1, LANES), lambda b, h, it, jt: (h, 0, 0)),
            pl.BlockSpec(bias.shape, lambda b, h, it, jt: (0, 0, 0)),
        ],
        out_specs=seq,
        scratch_shapes=[
            pltpu.VMEM((G, nq, LANES + ONES_ROWS, tq), BF16),
            pltpu.VMEM((G, nq, 2 * tq, LANES), BF16),
            pltpu.VMEM((2, G, tq, 2 * tq), F32),
            pltpu.VMEM((G, 1, 2 * tq), F32),
            pltpu.VMEM((G, LANES + ONES_ROWS, 2 * tq), F32),
            pltpu.VMEM((G, tq, 2 * tq), BF16),
            pltpu.VMEM((G, 1, 2 * tq), F32),
        ],
    )
    return pl.pallas_call(
        kern,
        grid_spec=grid_spec,
        out_shape=jax.ShapeDtypeStruct((B, T, D_C), BF16),
        compiler_params=_params(("parallel", "parallel")),
        name="diff_attn_prompt",
    )(i_tab, j_tab, lq, q, k, v, z, gn, bias)


def _paged_attn_kernel(pt_ref, lq_ref, qr_ref, *rest, P, lam_init, Tn):
    k_refs = rest[:P]
    v_refs = rest[P:2 * P]
    kn_ref, vn_ref, bias_ref, z_ref, gn_ref, o_ref, m_sc, l_sc, acc_sc = rest[2 * P:]
    j = pl.program_id(1)
    GR = 2 * Tn

    @pl.when(j == 0)
    def _():
        m_sc[...] = jnp.full(m_sc.shape, -jnp.inf, F32)
        l_sc[...] = jnp.zeros(l_sc.shape, F32)
        acc_sc[...] = jnp.zeros(acc_sc.shape, F32)

    qr = qr_ref[0]
    n_tok = k_refs[0].shape[2]

    def pages(kt_refs, vp_refs, bias):
        tiles = [_dot(qr, kt_ref[0].astype(BF16)) for kt_ref in kt_refs]
        if bias is not None:
            tiles = [x + bias for x in tiles]
        m_prev = m_sc[...]
        m_new = jnp.maximum(m_prev, jnp.max(functools.reduce(jnp.maximum, tiles), axis=1, keepdims=True))
        alpha = jnp.exp2(m_prev - m_new)
        ps = [jnp.exp2(x - m_new) for x in tiles]
        l_sc[...] = alpha * l_sc[...] + jnp.sum(functools.reduce(jnp.add, ps), axis=1, keepdims=True)
        m_sc[...] = m_new
        for hp in range(H_C):
            rs = slice(hp * GR, (hp + 1) * GR)
            p_h = jnp.concatenate([x[rs, :].astype(BF16) for x in ps], axis=1)
            v_h = jnp.concatenate(
                [vp_ref[0, pl.ds(hp, n_tok, stride=H_C), :].astype(BF16) for vp_ref in vp_refs], axis=0)
            acc_sc[rs, :] = alpha[rs, :] * acc_sc[rs, :] + _dot(p_h, v_h)

    pages(k_refs, v_refs, None)

    @pl.when(j == pl.num_programs(1) - 1)
    def _():
        pages([kn_ref], [vn_ref], bias_ref[...])
        out = acc_sc[...] / l_sc[...]
        lam = _lambda_value(lq_ref, lam_init)
        for hp in range(H_C):
            a = out[hp * GR:hp * GR + Tn] - lam * out[hp * GR + Tn:(hp + 1) * GR]
            y = _rms(a, gn_ref[hp:hp + 1, :]) * (1.0 - lam_init)
            z = z_ref[0, :, hp * LANES:(hp + 1) * LANES].astype(F32)
            o_ref[0, :, hp * LANES:(hp + 1) * LANES] = (y * _silu(z)).astype(o_ref.dtype)


def _paged_attn(page_table, lq, qrows, ck, cv, kn, vn, bias, z, gn, lam_init, P):
    B, R, _ = qrows.shape
    n_pages = page_table.shape[1]
    Tn = z.shape[1]
    kern = functools.partial(_paged_attn_kernel, P=P, lam_init=lam_init, Tn=Tn)

    def page_spec(arr, pi):
        return pl.BlockSpec((1,) + arr.shape[1:], lambda b, j, pt: (pt[b, j * P + pi], 0, 0))

    per_b = lambda shape: pl.BlockSpec((1,) + shape, lambda b, j, pt: (b,) + (0,) * len(shape))
    full = lambda shape: pl.BlockSpec(shape, lambda b, j, pt: (0,) * len(shape))
    grid_spec = pltpu.PrefetchScalarGridSpec(
        num_scalar_prefetch=1,
        grid=(B, n_pages // P),
        in_specs=[full(lq.shape), per_b(qrows.shape[1:])]
        + [page_spec(ck, pi) for pi in range(P)] + [page_spec(cv, pi) for pi in range(P)]
        + [per_b(kn.shape[1:]), per_b(vn.shape[1:]), full(bias.shape), per_b(z.shape[1:]), full(gn.shape)],
        out_specs=per_b((Tn, D_C)),
        scratch_shapes=[pltpu.VMEM((R, LANES), F32)] * 3,
    )
    return pl.pallas_call(
        kern,
        grid_spec=grid_spec,
        out_shape=jax.ShapeDtypeStruct((B, Tn, D_C), BF16),
        compiler_params=_params(("parallel", "arbitrary")),
        name="paged_diff_attn",
    )(page_table, lq, qrows, *([ck] * P), *([cv] * P), kn, vn, bias, z, gn)


def _rope_angles(pos, rot_dim, theta):
    half = rot_dim // 2
    inv = jnp.power(theta, -jnp.arange(half, dtype=F32) / half)
    return pos.astype(F32)[:, None] * inv[None, :]


def _retention_tables(L):
    lg = jnp.log(1.0 - jnp.power(2.0, -5.0 - jnp.arange(H_R, dtype=F32)))
    j = jnp.arange(L, dtype=F32)
    rel = j[:, None] - j[None, :]
    decay = jnp.where((rel >= 0)[None], jnp.exp(rel[None] * lg[:, None, None]), 0.0)
    q_in = jnp.exp((j + 1.0)[:, None] * lg)
    k_out = jnp.exp((L - 1.0 - j)[:, None] * lg)
    rtab = jnp.zeros((L, LANES), F32).at[:, 0:H_R].set(q_in).at[:, H_R:2 * H_R].set(k_out)
    cdec = jnp.zeros((1, LANES), F32).at[0, 0:H_R].set(jnp.exp(L * lg))
    return decay, rtab, cdec


def _partial_rope_tables(pos):
    ang = _rope_angles(pos, ROPE_DIM, ROPE_THETA)
    half = ROPE_DIM // 2
    cos, sin = jnp.cos(ang), jnp.sin(ang)
    T = pos.shape[0]
    ra = jnp.ones((T, HD_C), F32).at[:, 0:half].set(cos).at[:, half:ROPE_DIM].set(cos)
    rb = jnp.zeros((T, HD_C), F32).at[:, half:ROPE_DIM].set(sin)
    rc = jnp.zeros((T, HD_C), F32).at[:, 0:half].set(-sin)
    rep = LANES // HD_C
    return jnp.tile(ra, (1, rep)), jnp.tile(rb, (1, rep)), jnp.tile(rc, (1, rep))


def _forward(x, pos0, st_C, st_n, st_m, st_conv, st_S, attend, W):
    B, T, _ = x.shape
    M = B * T
    act_dtype = BF16 if T >= CHUNK else F32
    pos = pos0 + jnp.arange(T)

    x2 = x.reshape(M, D_MODEL)
    p, gates = _in_proj_a(x2, W["norm_pre"][0:1], W["w_main_a"], W["w_gate_a"], act_dtype)
    L = CHUNK if T % CHUNK == 0 else T
    TB = L
    ang = _rope_angles(pos, DK_R, RET_THETA)
    decay, rtab, cdec = _retention_tables(L)
    consts = (W["gbias"], W["conv_w"], W["conv_b"], W["gn_m"], W["gn_r"], decay, rtab, cdec)
    m0 = jnp.broadcast_to(st_m[..., None], st_m.shape + (LANES,))
    hcat, C, n, m, conv, S = _even_mixer(
        p.reshape(B, T, PROJ_MAIN_A), gates.reshape(B, T, LANES), jnp.cos(ang), jnp.sin(ang),
        consts, st_C, st_n, m0, st_conv, st_S, TB, L)
    x2 = _out_proj(hcat.reshape(M, D_M + D_R), W["w_out_a"], x2, W["norm_post"][0:1])

    tm = min(M, 512)
    n_rep = max(1, tm // T)
    ra, rb, rc = _partial_rope_tables(jnp.tile(pos, n_rep))
    q, kf, kb, vf, vb, z = _in_proj_c(x2, W["norm_pre"][1:2], W["w_in_c"], ra, rb, rc, tm)
    a = attend(q.reshape(B, T, D_QKC), kf.reshape(B, T, D_QKC), kb.reshape(B, T, D_QKC),
               vf.reshape(B, T, D_C), vb.reshape(B, T, D_C), z.reshape(B, T, D_C))
    y = _out_proj(a.reshape(M, D_C), W["w_out_c"], x2, W["norm_post"][1:2])
    return (y.reshape(B, T, D_MODEL), C[None], n[None], m[None, :, :, 0], conv[None], S[None],
            kf.reshape(1, B, T, 2 * H_C, HD_C), vf.reshape(1, B, T, H_C, 2 * HD_C))


def kernel(x_prompt, x_sample, state_mlstm_C, state_mlstm_n, state_mlstm_m, state_mlstm_conv, state_ret_S, cache_k, cache_v, page_table, norm_pre, norm_post, w_in_a, b_gate_i, b_gate_f, conv_w, conv_b, gn_mlstm, gn_ret, w_out_a, w_in_c, lambda_qk, gn_diff, w_out_c):
    g0 = 2 * D_QKM + 2 * D_M
    wa = w_in_a[0]
    W = {
        "norm_pre": norm_pre, "norm_post": norm_post,
        "w_main_a": jnp.concatenate([wa[:, :g0], wa[:, g0 + 2 * H_M:]], axis=1).astype(BF16),
        "w_gate_a": jnp.pad(wa[:, g0:g0 + 2 * H_M], ((0, 0), (0, LANES - 2 * H_M))).astype(BF16),
        "gbias": jnp.pad(jnp.concatenate([b_gate_i[0], b_gate_f[0]])[None], ((0, 0), (0, LANES - 2 * H_M))),
        "conv_w": conv_w[0], "conv_b": conv_b[0][None],
        "gn_m": gn_mlstm[0].reshape(1, D_M), "gn_r": gn_ret[0].reshape(1, D_R),
        "w_out_a": w_out_a[0].astype(BF16),
        "w_in_c": w_in_c[0].astype(BF16),
        "w_out_c": w_out_c[0].astype(BF16),
    }
    lam_init = 0.8 - 0.6 * math.exp(-0.3 * 1)
    lq = lambda_qk[0]
    gn_d = gn_diff[0]

    Bp, Tp, _ = x_prompt.shape

    def attend_prompt(q, kf, kb, vf, vb, z):
        return _diff_attn_prompt(lq, q, kb, vb, z, gn_d[:, None, :], lam_init, tq=256)

    outs_p = _forward(
        x_prompt, 0,
        jnp.zeros((Bp, H_M, DK_M, DV_M), F32), jnp.zeros((Bp, H_M, DK_M), F32),
        jnp.zeros((Bp, H_M), F32), jnp.zeros((Bp, CONV_W - 1, 2 * D_QKM), F32),
        jnp.zeros((Bp, H_R, DK_R, DV_R), F32), attend_prompt, W)

    Bs, Ts, _ = x_sample.shape
    n_pool, page_size = cache_k.shape[1], cache_k.shape[2]
    past = page_table.shape[1] * page_size
    ck = jnp.transpose(cache_k[0], (0, 2, 3, 1)).reshape(n_pool, D_QKC, page_size)
    cv = cache_v[0].reshape(n_pool, page_size * H_C, 2 * HD_C)

    def attend_sample(q, kf, kb, vf, vb, z):
        head_of_lane = jnp.arange(D_QKC) // HD_C
        sel = (head_of_lane[None, :] == jnp.arange(2 * H_C)[:, None])
        qrows = jnp.where(sel[None, :, None, :], q[:, None, :, :], jnp.zeros((), q.dtype))
        qrows = qrows.reshape(Bs, 2 * H_C * Ts, D_QKC)
        kn = jnp.pad(jnp.transpose(kf, (0, 2, 1)), ((0, 0), (0, 0), (0, page_size - Ts)))
        vn = jnp.pad(vf, ((0, 0), (0, page_size - Ts), (0, 0))).reshape(Bs, page_size * H_C, 2 * HD_C)
        tok = jnp.arange(2 * H_C * Ts) % Ts
        col = jnp.arange(page_size)
        bias = jnp.where(col[None, :] <= tok[:, None], 0.0, -jnp.inf).astype(F32)
        return _paged_attn(page_table, lq, qrows, ck, cv, kn, vn, bias, z, gn_d, lam_init, P=8)

    outs_s = _forward(x_sample, past, state_mlstm_C[0], state_mlstm_n[0], state_mlstm_m[0],
                      state_mlstm_conv[0], state_ret_S[0], attend_sample, W)

    return (outs_p[0], outs_s[0]) + tuple(outs_p[1:]) + tuple(outs_s[1:])
```

```python
import functools
import math

import jax
import jax.numpy as jnp
from jax import lax
from jax.experimental import pallas as pl
from jax.experimental.pallas import tpu as pltpu

F32 = jnp.float32
BF16 = jnp.bfloat16

D_MODEL = 1024
H_M, DK_M, DV_M = 4, 256, 256
D_QKM = H_M * DK_M
D_M = H_M * DV_M
CONV_W = 4
H_R, DK_R, DV_R = 4, 256, 256
D_QKR = H_R * DK_R
D_R = H_R * DV_R
RET_THETA = 10000.0
H_C, HD_C = 8, 64
D_QKC = 2 * H_C * HD_C
D_C = H_C * 2 * HD_C
ROPE_THETA = 500000.0
ROPE_DIM = HD_C // 4
CHUNK = 256
EPS = 1e-6
PROJ_MAIN_A = 2 * D_QKM + 2 * D_M + 2 * D_QKR + 2 * D_R
LANES = 128
LOG2E = math.log2(math.e)
VMEM_LIMIT = 52 * 1024 * 1024

OFF_QKM, OFF_VM, OFF_ZM = 0, 2 * D_QKM, 2 * D_QKM + D_M
OFF_QR = 2 * D_QKM + 2 * D_M
OFF_KR, OFF_VR, OFF_ZR = OFF_QR + D_QKR, OFF_QR + 2 * D_QKR, OFF_QR + 2 * D_QKR + D_R


def _params(sem):
    return pltpu.CompilerParams(dimension_semantics=sem, vmem_limit_bytes=VMEM_LIMIT)


def _rms(x, g):
    return x * lax.rsqrt(jnp.mean(x * x, axis=-1, keepdims=True) + EPS) * g


def _silu(x):
    return x * jax.nn.sigmoid(x)


def _dot(a, b):
    return jnp.dot(a, b, preferred_element_type=F32)


def _dot_nt(a, b):
    return lax.dot_general(a, b, (((1,), (1,)), ((), ())), preferred_element_type=F32)


def _dot_tn(a, b):
    return lax.dot_general(a, b, (((0,), (0,)), ((), ())), preferred_element_type=F32)


def _in_proj_a_kernel(x_ref, g_ref, w_ref, wg_ref, o_ref, og_ref, xn_ref):
    @pl.when(pl.program_id(1) == 0)
    def _():
        xn = _rms(x_ref[...], g_ref[...]).astype(BF16)
        xn_ref[...] = xn
        og_ref[...] = _dot(xn, wg_ref[...])

    o_ref[...] = _dot(xn_ref[...], w_ref[...]).astype(o_ref.dtype)


def _in_proj_a(x, g, w_main, w_gate, out_dtype):
    M, K = x.shape
    N = w_main.shape[1]
    tm = min(M, 1024)
    tn = 1024
    return pl.pallas_call(
        _in_proj_a_kernel,
        grid=(M // tm, N // tn),
        in_specs=[
            pl.BlockSpec((tm, K), lambda i, j: (i, 0)),
            pl.BlockSpec((1, K), lambda i, j: (0, 0)),
            pl.BlockSpec((K, tn), lambda i, j: (0, j)),
            pl.BlockSpec((K, LANES), lambda i, j: (0, 0)),
        ],
        out_specs=[
            pl.BlockSpec((tm, tn), lambda i, j: (i, j)),
            pl.BlockSpec((tm, LANES), lambda i, j: (i, 0)),
        ],
        out_shape=[
            jax.ShapeDtypeStruct((M, N), out_dtype),
            jax.ShapeDtypeStruct((M, LANES), F32),
        ],
        scratch_shapes=[pltpu.VMEM((tm, K), BF16)],
        compiler_params=_params(("parallel", "arbitrary")),
        name="in_proj_even",
    )(x, g, w_main, w_gate)


def _even_mixer_kernel(p_ref, g_ref, cos_ref, sin_ref, gb_ref, cw_ref, cb_ref, gnm_ref, gnr_ref,
                       dec_ref, rtab_ref, cdec_ref, C0_ref, n0_ref, m0_ref, cv0_ref, S0_ref,
                       h_ref, C_ref, n_ref, m_ref, cv_ref, S_ref, xbuf, qkc, qkr, *, TB, L):
    t = pl.program_id(1)
    HIST = 8

    @pl.when(t == 0)
    def _():
        C_ref[...] = C0_ref[...]
        n_ref[...] = n0_ref[...]
        m_ref[...] = m0_ref[...]
        S_ref[...] = S0_ref[...]
        xbuf[HIST - (CONV_W - 1):HIST, :] = cv0_ref[0]

    @pl.when(t > 0)
    def _():
        xbuf[0:HIST, :] = xbuf[TB:TB + HIST, :]

    for cg in range(2 * D_QKM // 256):
        cs = slice(cg * 256, (cg + 1) * 256)
        x = p_ref[0, :, cs].astype(F32)
        xbuf[HIST:HIST + TB, cs] = x
        acc = cb_ref[:, cs] + xbuf[HIST - 3:HIST - 3 + TB, cs] * cw_ref[0:1, cs]
        acc = acc + xbuf[HIST - 2:HIST - 2 + TB, cs] * cw_ref[1:2, cs]
        acc = acc + xbuf[HIST - 1:HIST - 1 + TB, cs] * cw_ref[2:3, cs]
        acc = acc + x * cw_ref[3:4, cs]
        y = _silu(acc)
        if cg < D_QKM // 256:
            y = y * (DK_M ** -0.5)
        qkc[0:TB, cs] = y
    cv_ref[0] = xbuf[HIST + TB - (CONV_W - 1):HIST + TB, :]

    cos = cos_ref[...]
    sin = sin_ref[...]
    half = DK_R // 2
    for which, off, scale in ((0, OFF_QR, 1.0), (1, OFF_KR, DK_R ** -0.5)):
        for h in range(H_R):
            x1 = p_ref[0, :, off + h * DK_R:off + h * DK_R + half].astype(F32)
            x2 = p_ref[0, :, off + h * DK_R + half:off + (h + 1) * DK_R].astype(F32)
            base = which * D_QKR + h * DK_R
            qkr[0:TB, base:base + half] = (x1 * cos - x2 * sin) * scale
            qkr[0:TB, base + half:base + DK_R] = (x1 * sin + x2 * cos) * scale

    def chunk(rows):
        gates = g_ref[0, rows, :] + gb_ref[...]
        ti = lax.broadcasted_iota(jnp.int32, (L, L), 0)
        si = lax.broadcasted_iota(jnp.int32, (L, L), 1)
        eye = ti == si
        low = si <= ti
        upp = ti <= si
        for h in range(H_M):
            ig = gates[:, h:h + 1]
            lf = jax.nn.log_sigmoid(gates[:, H_M + h:H_M + h + 1])
            lf_r = jnp.sum(jnp.where(eye, lf, 0.0), axis=0, keepdims=True)
            i_r = jnp.sum(jnp.where(eye, ig, 0.0), axis=0, keepdims=True)
            b_c = jnp.sum(jnp.where(low, lf_r, 0.0), axis=1, keepdims=True)
            b_r = jnp.sum(jnp.where(upp, lf, 0.0), axis=0, keepdims=True)
            m_prev = m_ref[0, h:h + 1, 0:1]
            inter = b_c + m_prev
            Dm = jnp.where(low, b_c - b_r + i_r, -jnp.inf)
            m_t = jnp.maximum(inter, jnp.max(Dm, axis=1, keepdims=True))
            w = jnp.exp(Dm - m_t)
            s_int = jnp.exp(inter - m_t)
            q = qkc[rows, h * DK_M:(h + 1) * DK_M]
            k = qkc[rows, D_QKM + h * DK_M:D_QKM + (h + 1) * DK_M]
            vb = p_ref[0, rows, OFF_VM + h * DV_M:OFF_VM + (h + 1) * DV_M].astype(BF16)
            qb = q.astype(BF16)
            qk = _dot_nt(qb, k.astype(BF16)) * w
            C = C_ref[0, h]
            nvec = n_ref[0, h:h + 1, :]
            num = _dot(qk.astype(BF16), vb) + s_int * _dot(qb, C.astype(BF16))
            den = (jnp.sum(qk, axis=1, keepdims=True)
                   + s_int * jnp.sum(q * nvec, axis=1, keepdims=True))
            hh = num / jnp.maximum(jnp.abs(den), jnp.exp(-m_t))
            y = _rms(hh, gnm_ref[:, h * DV_M:(h + 1) * DV_M])
            z = p_ref[0, rows, OFF_ZM + h * DV_M:OFF_ZM + (h + 1) * DV_M].astype(F32)
            h_ref[0, rows, h * DV_M:(h + 1) * DV_M] = (y * _silu(z)).astype(h_ref.dtype)
            bL = b_c[L - 1:L, :]
            dec = bL - b_c + ig
            m_new = jnp.maximum(bL + m_prev, jnp.max(dec, axis=0, keepdims=True))
            ws = jnp.exp(dec - m_new)
            sc = jnp.exp(bL + m_prev - m_new)
            kw = k * ws
            C_ref[0, h] = sc * C + _dot_tn(kw.astype(BF16), vb)
            n_ref[0, h:h + 1, :] = sc * nvec + jnp.sum(kw, axis=0, keepdims=True)
            m_ref[0, h:h + 1, :] = jnp.broadcast_to(m_new, (1, LANES))
        for h in range(H_R):
            q = qkr[rows, h * DK_R:(h + 1) * DK_R]
            k = qkr[rows, D_QKR + h * DK_R:D_QKR + (h + 1) * DK_R]
            vb = p_ref[0, rows, OFF_VR + h * DV_R:OFF_VR + (h + 1) * DV_R].astype(BF16)
            qk = _dot_nt(q.astype(BF16), k.astype(BF16)) * dec_ref[h]
            S = S_ref[0, h]
            q_in = rtab_ref[:, h:h + 1]
            k_out = rtab_ref[:, H_R + h:H_R + h + 1]
            o = _dot(qk.astype(BF16), vb) + _dot((q * q_in).astype(BF16), S.astype(BF16))
            S_ref[0, h] = cdec_ref[:, h:h + 1] * S + _dot_tn((k * k_out).astype(BF16), vb)
            y = _rms(o, gnr_ref[:, h * DV_R:(h + 1) * DV_R])
            z = p_ref[0, rows, OFF_ZR + h * DV_R:OFF_ZR + (h + 1) * DV_R].astype(F32)
            h_ref[0, rows, D_M + h * DV_R:D_M + (h + 1) * DV_R] = (y * _silu(z)).astype(h_ref.dtype)

    n_chunks = TB // L
    if n_chunks == 1:
        chunk(slice(0, L))
    else:
        def body(c, carry):
            chunk(pl.ds(pl.multiple_of(c * L, L), L))
            return carry
        lax.fori_loop(0, n_chunks, body, 0)


def _even_mixer(p, gates, cos_r, sin_r, consts, C0, n0, m0, cv0, S0, TB, L):
    B, T, _ = p.shape
    gbias, cw, cb, gnm, gnr, decay, rtab, cdec = consts
    nt = T // TB
    tb8 = -(-TB // 8) * 8
    full = lambda shape: pl.BlockSpec(shape, lambda b, t: (0,) * len(shape))
    per_b = lambda shape: pl.BlockSpec((1,) + shape, lambda b, t: (b,) + (0,) * len(shape))
    kern = functools.partial(_even_mixer_kernel, TB=TB, L=L)
    return pl.pallas_call(
        kern,
        grid=(B, nt),
        in_specs=[
            pl.BlockSpec((1, TB, PROJ_MAIN_A), lambda b, t: (b, t, 0)),
            pl.BlockSpec((1, TB, LANES), lambda b, t: (b, t, 0)),
            pl.BlockSpec((TB, LANES), lambda b, t: (t, 0)),
            pl.BlockSpec((TB, LANES), lambda b, t: (t, 0)),
            full((1, LANES)), full((CONV_W, 2 * D_QKM)), full((1, 2 * D_QKM)),
            full((1, D_M)), full((1, D_R)), full((H_R, L, L)), full((L, LANES)), full((1, LANES)),
            per_b((H_M, DK_M, DV_M)), per_b((H_M, DK_M)), per_b((H_M, LANES)),
            per_b((CONV_W - 1, 2 * D_QKM)), per_b((H_R, DK_R, DV_R)),
        ],
        out_specs=[
            pl.BlockSpec((1, TB, D_M + D_R), lambda b, t: (b, t, 0)),
            per_b((H_M, DK_M, DV_M)), per_b((H_M, DK_M)), per_b((H_M, LANES)),
            per_b((CONV_W - 1, 2 * D_QKM)), per_b((H_R, DK_R, DV_R)),
        ],
        out_shape=[
            jax.ShapeDtypeStruct((B, T, D_M + D_R), BF16),
            jax.ShapeDtypeStruct((B, H_M, DK_M, DV_M), F32),
            jax.ShapeDtypeStruct((B, H_M, DK_M), F32),
            jax.ShapeDtypeStruct((B, H_M, LANES), F32),
            jax.ShapeDtypeStruct((B, CONV_W - 1, 2 * D_QKM), F32),
            jax.ShapeDtypeStruct((B, H_R, DK_R, DV_R), F32),
        ],
        scratch_shapes=[
            pltpu.VMEM((tb8 + 8, 2 * D_QKM), F32),
            pltpu.VMEM((tb8, 2 * D_QKM), F32),
            pltpu.VMEM((tb8, 2 * D_QKR), F32),
        ],
        compiler_params=_params(("parallel", "arbitrary")),
        name="even_mixer",
    )(p, gates, cos_r, sin_r, gbias, cw, cb, gnm, gnr, decay, rtab, cdec, C0, n0, m0, cv0, S0)


def _out_proj_kernel(h_ref, w_ref, x_ref, g_ref, y_ref):
    o = _dot(h_ref[...], w_ref[...])
    y_ref[...] = x_ref[...] + _rms(o, g_ref[...])


def _out_proj(h, w, x, g):
    M, K = h.shape
    N = w.shape[1]
    tm = min(M, 512)
    return pl.pallas_call(
        _out_proj_kernel,
        grid=(M // tm,),
        in_specs=[
            pl.BlockSpec((tm, K), lambda i: (i, 0)),
            pl.BlockSpec((K, N), lambda i: (0, 0)),
            pl.BlockSpec((tm, N), lambda i: (i, 0)),
            pl.BlockSpec((1, N), lambda i: (0, 0)),
        ],
        out_specs=pl.BlockSpec((tm, N), lambda i: (i, 0)),
        out_shape=jax.ShapeDtypeStruct((M, N), F32),
        compiler_params=_params(("parallel",)),
        name="out_proj",
    )(h, w, x, g)


def _in_proj_c_kernel(x_ref, g_ref, w_ref, ra_ref, rb_ref, rc_ref,
                      q_ref, kf_ref, kb_ref, vf_ref, vb_ref, z_ref):
    xn = _rms(x_ref[...], g_ref[...]).astype(BF16)
    ra = ra_ref[...]
    rb = rb_ref[...]
    rc = rc_ref[...]
    half = ROPE_DIM // 2

    def rope(y):
        outs = []
        for gi in range(y.shape[1] // LANES):
            yg = y[:, gi * LANES:(gi + 1) * LANES]
            outs.append(yg * ra + pltpu.roll(yg, half, 1) * rb + pltpu.roll(yg, LANES - half, 1) * rc)
        return jnp.concatenate(outs, axis=1)

    q = rope(_dot(xn, w_ref[:, 0:D_QKC]))
    q_ref[...] = (q * (HD_C ** -0.5 * LOG2E)).astype(BF16)
    k = rope(_dot(xn, w_ref[:, D_QKC:2 * D_QKC]))
    kf_ref[...] = k
    kb_ref[...] = k.astype(BF16)
    v = _dot(xn, w_ref[:, 2 * D_QKC:2 * D_QKC + D_C])
    vf_ref[...] = v
    vb_ref[...] = v.astype(BF16)
    z_ref[...] = _dot(xn, w_ref[:, 2 * D_QKC + D_C:]).astype(BF16)


def _in_proj_c(x, g, w, ra, rb, rc, tm):
    M, K = x.shape
    n_rt = ra.shape[0] // tm
    row = lambda n: pl.BlockSpec((tm, n), lambda i: (i, 0))
    tab = pl.BlockSpec((tm, LANES), lambda i: (i % n_rt, 0))
    return pl.pallas_call(
        _in_proj_c_kernel,
        grid=(M // tm,),
        in_specs=[
            row(K),
            pl.BlockSpec((1, K), lambda i: (0, 0)),
            pl.BlockSpec(w.shape, lambda i: (0, 0)),
            tab, tab, tab,
        ],
        out_specs=[row(D_QKC), row(D_QKC), row(D_QKC), row(D_C), row(D_C), row(D_C)],
        out_shape=[
            jax.ShapeDtypeStruct((M, D_QKC), BF16),
            jax.ShapeDtypeStruct((M, D_QKC), F32),
            jax.ShapeDtypeStruct((M, D_QKC), BF16),
            jax.ShapeDtypeStruct((M, D_C), F32),
            jax.ShapeDtypeStruct((M, D_C), BF16),
            jax.ShapeDtypeStruct((M, D_C), BF16),
        ],
        compiler_params=_params(("parallel",)),
        name="in_proj_odd",
    )(x, g, w, ra, rb, rc)


def _lambda_value(lq_ref, lam_init):
    lf = lq_ref[...]
    l1 = jnp.sum(lf[0:1] * lf[1:2], axis=1, keepdims=True)
    l2 = jnp.sum(lf[2:3] * lf[3:4], axis=1, keepdims=True)
    return jnp.exp(l1) - jnp.exp(l2) + lam_init


def _diff_attn_kernel(it_ref, jt_ref, lq_ref, q_ref, k_ref, v_ref, z_ref, gn_ref, bias_ref, o_ref,
                      vt_sc, q2_sc, s_sc, m_sc, l_sc, acc_sc, *, tq, n_steps, lam_init, G):
    nb = v_ref.shape[1] // tq
    lane = lax.broadcasted_iota(jnp.int32, (tq, LANES), 1)

    def blk(idx):
        return pl.ds(pl.multiple_of(idx * tq, tq), tq)

    def head(g):
        return slice(g * LANES, (g + 1) * LANES)

    for g in range(G):
        for jb in range(nb):
            rows = slice(jb * tq, (jb + 1) * tq)
            vt_sc[g, jb] = v_ref[0, rows, head(g)].astype(F32).T.astype(BF16)
            q = q_ref[0, rows, head(g)]
            zero = jnp.zeros_like(q)
            q2_sc[g, jb, 0:tq, :] = jnp.where(lane < HD_C, q, zero)
            q2_sc[g, jb, tq:2 * tq, :] = jnp.where(lane >= HD_C, q, zero)

    def scores(t, slot, first=False):
        for g in range(G):
            if first:
                s_sc[slot, g] = _dot_nt(k_ref[0, 0:tq, head(g)], q2_sc[g, 0])
            else:
                s_sc[slot, g] = _dot_nt(k_ref[0, blk(jt_ref[t]), head(g)], q2_sc[g, it_ref[t]])

    scores(0, 0, first=True)

    def step(t, slot):
        i = it_ref[t]
        j = jt_ref[t]

        @pl.when(j == 0)
        def _():
            m_sc[...] = jnp.full(m_sc.shape, -jnp.inf, F32)
            l_sc[...] = jnp.zeros(l_sc.shape, F32)
            acc_sc[...] = jnp.zeros(acc_sc.shape, F32)

        scores(t + 1, 1 - slot)
        bias = bias_ref[(j == i).astype(jnp.int32)]
        for g in range(G):
            s = s_sc[slot, g] + bias
            m_prev = m_sc[g]
            m_new = jnp.maximum(m_prev, jnp.max(s, axis=0, keepdims=True))
            p = jnp.exp2(s - m_new)
            alpha = jnp.exp2(m_prev - m_new)
            l_sc[g] = alpha * l_sc[g] + jnp.sum(p, axis=0, keepdims=True)
            acc_sc[g] = alpha * acc_sc[g] + _dot(vt_sc[g, j], p.astype(BF16))
            m_sc[g] = m_new

        @pl.when(j == i)
        def _():
            lam = _lambda_value(lq_ref, lam_init)
            for g in range(G):
                out_t = acc_sc[g] / l_sc[g]
                a_t = out_t[:, 0:tq] - lam * out_t[:, tq:2 * tq]
                a_t = a_t * lax.rsqrt(jnp.mean(a_t * a_t, axis=0, keepdims=True) + EPS)
                y = a_t.T * gn_ref[g] * (1.0 - lam_init)
                z = z_ref[0, blk(i), head(g)].astype(F32)
                o_ref[0, blk(i), head(g)] = (y * _silu(z)).astype(o_ref.dtype)

    def body(u, carry):
        step(2 * u, 0)
        step(2 * u + 1, 1)
        return carry

    assert n_steps % 2 == 0
    lax.fori_loop(0, n_steps // 2, body, 0)


def _diff_attn_prompt(lq, q, k, v, z, gn, lam_init, tq, G):
    B, T, _ = q.shape
    nq = T // tq
    pairs = [(i, j) for i in range(nq) for j in range(i + 1)]
    n_steps = len(pairs)
    pairs.append((0, 0))
    i_tab = jnp.asarray([p[0] for p in pairs], jnp.int32)
    j_tab = jnp.asarray([p[1] for p in pairs], jnp.int32)
    qpos = jnp.arange(2 * tq) % tq
    causal = jnp.where(jnp.arange(tq)[:, None] <= qpos[None, :], 0.0, -jnp.inf).astype(F32)
    bias = jnp.stack([jnp.zeros_like(causal), causal])
    kern = functools.partial(_diff_attn_kernel, tq=tq, n_steps=n_steps, lam_init=lam_init, G=G)
    seq = pl.BlockSpec((1, T, G * LANES), lambda b, h, it, jt: (b, 0, h))
    grid_spec = pltpu.PrefetchScalarGridSpec(
        num_scalar_prefetch=2,
        grid=(B, H_C // G),
        in_specs=[
            pl.BlockSpec(lq.shape, lambda b, h, it, jt: (0, 0)),
            seq, seq, seq, seq,
            pl.BlockSpec((G, 1, LANES), lambda b, h, it, jt: (h, 0, 0)),
            pl.BlockSpec(bias.shape, lambda b, h, it, jt: (0, 0, 0)),
        ],
        out_specs=seq,
        scratch_shapes=[
            pltpu.VMEM((G, nq, LANES, tq), BF16),
            pltpu.VMEM((G, nq, 2 * tq, LANES), BF16),
            pltpu.VMEM((2, G, tq, 2 * tq), F32),
            pltpu.VMEM((G, 1, 2 * tq), F32),
            pltpu.VMEM((G, 1, 2 * tq), F32),
            pltpu.VMEM((G, LANES, 2 * tq), F32),
        ],
    )
    return pl.pallas_call(
        kern,
        grid_spec=grid_spec,
        out_shape=jax.ShapeDtypeStruct((B, T, D_C), BF16),
        compiler_params=_params(("parallel", "parallel")),
        name="diff_attn_prompt",
    )(i_tab, j_tab, lq, q, k, v, z, gn, bias)


def _paged_attn_kernel(pt_ref, lq_ref, qr_ref, *rest, P, lam_init, Tn):
    k_refs = rest[:P]
    v_refs = rest[P:2 * P]
    kn_ref, vn_ref, bias_ref, z_ref, gn_ref, o_ref, m_sc, l_sc, acc_sc = rest[2 * P:]
    j = pl.program_id(1)
    GR = 2 * Tn

    @pl.when(j == 0)
    def _():
        m_sc[...] = jnp.full(m_sc.shape, -jnp.inf, F32)
        l_sc[...] = jnp.zeros(l_sc.shape, F32)
        acc_sc[...] = jnp.zeros(acc_sc.shape, F32)

    qr = qr_ref[0]
    n_tok = k_refs[0].shape[2]

    def pages(kt_refs, vp_refs, bias):
        tiles = [_dot(qr, kt_ref[0].astype(BF16)) for kt_ref in kt_refs]
        if bias is not None:
            tiles = [x + bias for x in tiles]
        m_prev = m_sc[...]
        m_new = jnp.maximum(m_prev, jnp.max(functools.reduce(jnp.maximum, tiles), axis=1, keepdims=True))
        alpha = jnp.exp2(m_prev - m_new)
        ps = [jnp.exp2(x - m_new) for x in tiles]
        l_sc[...] = alpha * l_sc[...] + jnp.sum(functools.reduce(jnp.add, ps), axis=1, keepdims=True)
        m_sc[...] = m_new
        for hp in range(H_C):
            rs = slice(hp * GR, (hp + 1) * GR)
            p_h = jnp.concatenate([x[rs, :].astype(BF16) for x in ps], axis=1)
            v_h = jnp.concatenate(
                [vp_ref[0, pl.ds(hp, n_tok, stride=H_C), :].astype(BF16) for vp_ref in vp_refs], axis=0)
            acc_sc[rs, :] = alpha[rs, :] * acc_sc[rs, :] + _dot(p_h, v_h)

    pages(k_refs, v_refs, None)

    @pl.when(j == pl.num_programs(1) - 1)
    def _():
        pages([kn_ref], [vn_ref], bias_ref[...])
        out = acc_sc[...] / l_sc[...]
        lam = _lambda_value(lq_ref, lam_init)
        for hp in range(H_C):
            a = out[hp * GR:hp * GR + Tn] - lam * out[hp * GR + Tn:(hp + 1) * GR]
            y = _rms(a, gn_ref[hp:hp + 1, :]) * (1.0 - lam_init)
            z = z_ref[0, :, hp * LANES:(hp + 1) * LANES].astype(F32)
            o_ref[0, :, hp * LANES:(hp + 1) * LANES] = (y * _silu(z)).astype(o_ref.dtype)


def _paged_attn(page_table, lq, qrows, ck, cv, kn, vn, bias, z, gn, lam_init, P):
    B, R, _ = qrows.shape
    n_pages = page_table.shape[1]
    Tn = z.shape[1]
    kern = functools.partial(_paged_attn_kernel, P=P, lam_init=lam_init, Tn=Tn)

    def page_spec(arr, pi):
        return pl.BlockSpec((1,) + arr.shape[1:], lambda b, j, pt: (pt[b, j * P + pi], 0, 0))

    per_b = lambda shape: pl.BlockSpec((1,) + shape, lambda b, j, pt: (b,) + (0,) * len(shape))
    full = lambda shape: pl.BlockSpec(shape, lambda b, j, pt: (0,) * len(shape))
    grid_spec = pltpu.PrefetchScalarGridSpec(
        num_scalar_prefetch=1,
        grid=(B, n_pages // P),
        in_specs=[full(lq.shape), per_b(qrows.shape[1:])]
        + [page_spec(ck, pi) for pi in range(P)] + [page_spec(cv, pi) for pi in range(P)]
        + [per_b(kn.shape[1:]), per_b(vn.shape[1:]), full(bias.shape), per_b(z.shape[1:]), full(gn.shape)],
        out_specs=per_b((Tn, D_C)),
        scratch_shapes=[pltpu.VMEM((R, LANES), F32)] * 3,
    )
    return pl.pallas_call(
        kern,
        grid_spec=grid_spec,
        out_shape=jax.ShapeDtypeStruct((B, Tn, D_C), BF16),
        compiler_params=_params(("parallel", "arbitrary")),
        name="paged_diff_attn",
    )(page_table, lq, qrows, *([ck] * P), *([cv] * P), kn, vn, bias, z, gn)


def _rope_angles(pos, rot_dim, theta):
    half = rot_dim // 2
    inv = jnp.power(theta, -jnp.arange(half, dtype=F32) / half)
    return pos.astype(F32)[:, None] * inv[None, :]


def _retention_tables(L):
    lg = jnp.log(1.0 - jnp.power(2.0, -5.0 - jnp.arange(H_R, dtype=F32)))
    j = jnp.arange(L, dtype=F32)
    rel = j[:, None] - j[None, :]
    decay = jnp.where((rel >= 0)[None], jnp.exp(rel[None] * lg[:, None, None]), 0.0)
    q_in = jnp.exp((j + 1.0)[:, None] * lg)
    k_out = jnp.exp((L - 1.0 - j)[:, None] * lg)
    rtab = jnp.zeros((L, LANES), F32).at[:, 0:H_R].set(q_in).at[:, H_R:2 * H_R].set(k_out)
    cdec = jnp.zeros((1, LANES), F32).at[0, 0:H_R].set(jnp.exp(L * lg))
    return decay, rtab, cdec


def _partial_rope_tables(pos):
    ang = _rope_angles(pos, ROPE_DIM, ROPE_THETA)
    half = ROPE_DIM // 2
    cos, sin = jnp.cos(ang), jnp.sin(ang)
    T = pos.shape[0]
    ra = jnp.ones((T, HD_C), F32).at[:, 0:half].set(cos).at[:, half:ROPE_DIM].set(cos)
    rb = jnp.zeros((T, HD_C), F32).at[:, half:ROPE_DIM].set(sin)
    rc = jnp.zeros((T, HD_C), F32).at[:, 0:half].set(-sin)
    rep = LANES // HD_C
    return jnp.tile(ra, (1, rep)), jnp.tile(rb, (1, rep)), jnp.tile(rc, (1, rep))


def _forward(x, pos0, st_C, st_n, st_m, st_conv, st_S, attend, W):
    B, T, _ = x.shape
    M = B * T
    act_dtype = BF16 if T >= CHUNK else F32
    pos = pos0 + jnp.arange(T)

    x2 = x.reshape(M, D_MODEL)
    p, gates = _in_proj_a(x2, W["norm_pre"][0:1], W["w_main_a"], W["w_gate_a"], act_dtype)
    L = CHUNK if T % CHUNK == 0 else T
    TB = L
    ang = _rope_angles(pos, DK_R, RET_THETA)
    decay, rtab, cdec = _retention_tables(L)
    consts = (W["gbias"], W["conv_w"], W["conv_b"], W["gn_m"], W["gn_r"], decay, rtab, cdec)
    m0 = jnp.broadcast_to(st_m[..., None], st_m.shape + (LANES,))
    hcat, C, n, m, conv, S = _even_mixer(
        p.reshape(B, T, PROJ_MAIN_A), gates.reshape(B, T, LANES), jnp.cos(ang), jnp.sin(ang),
        consts, st_C, st_n, m0, st_conv, st_S, TB, L)
    x2 = _out_proj(hcat.reshape(M, D_M + D_R), W["w_out_a"], x2, W["norm_post"][0:1])

    tm = min(M, 512)
    n_rep = max(1, tm // T)
    ra, rb, rc = _partial_rope_tables(jnp.tile(pos, n_rep))
    q, kf, kb, vf, vb, z = _in_proj_c(x2, W["norm_pre"][1:2], W["w_in_c"], ra, rb, rc, tm)
    a = attend(q.reshape(B, T, D_QKC), kf.reshape(B, T, D_QKC), kb.reshape(B, T, D_QKC),
               vf.reshape(B, T, D_C), vb.reshape(B, T, D_C), z.reshape(B, T, D_C))
    y = _out_proj(a.reshape(M, D_C), W["w_out_c"], x2, W["norm_post"][1:2])
    return (y.reshape(B, T, D_MODEL), C[None], n[None], m[None, :, :, 0], conv[None], S[None],
            kf.reshape(1, B, T, 2 * H_C, HD_C), vf.reshape(1, B, T, H_C, 2 * HD_C))


def kernel(x_prompt, x_sample, state_mlstm_C, state_mlstm_n, state_mlstm_m, state_mlstm_conv, state_ret_S, cache_k, cache_v, page_table, norm_pre, norm_post, w_in_a, b_gate_i, b_gate_f, conv_w, conv_b, gn_mlstm, gn_ret, w_out_a, w_in_c, lambda_qk, gn_diff, w_out_c):
    g0 = 2 * D_QKM + 2 * D_M
    wa = w_in_a[0]
    W = {
        "norm_pre": norm_pre, "norm_post": norm_post,
        "w_main_a": jnp.concatenate([wa[:, :g0], wa[:, g0 + 2 * H_M:]], axis=1).astype(BF16),
        "w_gate_a": jnp.pad(wa[:, g0:g0 + 2 * H_M], ((0, 0), (0, LANES - 2 * H_M))).astype(BF16),
        "gbias": jnp.pad(jnp.concatenate([b_gate_i[0], b_gate_f[0]])[None], ((0, 0), (0, LANES - 2 * H_M))),
        "conv_w": conv_w[0], "conv_b": conv_b[0][None],
        "gn_m": gn_mlstm[0].reshape(1, D_M), "gn_r": gn_ret[0].reshape(1, D_R),
        "w_out_a": w_out_a[0].astype(BF16),
        "w_in_c": w_in_c[0].astype(BF16),
        "w_out_c": w_out_c[0].astype(BF16),
    }
    lam_init = 0.8 - 0.6 * math.exp(-0.3 * 1)
    lq = lambda_qk[0]
    gn_d = gn_diff[0]

    Bp, Tp, _ = x_prompt.shape

    def attend_prompt(q, kf, kb, vf, vb, z):
        return _diff_attn_prompt(lq, q, kb, vb, z, gn_d[:, None, :], lam_init, tq=256, G=2)

    outs_p = _forward(
        x_prompt, 0,
        jnp.zeros((Bp, H_M, DK_M, DV_M), F32), jnp.zeros((Bp, H_M, DK_M), F32),
        jnp.zeros((Bp, H_M), F32), jnp.zeros((Bp, CONV_W - 1, 2 * D_QKM), F32),
        jnp.zeros((Bp, H_R, DK_R, DV_R), F32), attend_prompt, W)

    Bs, Ts, _ = x_sample.shape
    n_pool, page_size = cache_k.shape[1], cache_k.shape[2]
    past = page_table.shape[1] * page_size
    ck = jnp.transpose(cache_k[0], (0, 2, 3, 1)).reshape(n_pool, D_QKC, page_size)
    cv = cache_v[0].reshape(n_pool, page_size * H_C, 2 * HD_C)

    def attend_sample(q, kf, kb, vf, vb, z):
        head_of_lane = jnp.arange(D_QKC) // HD_C
        sel = (head_of_lane[None, :] == jnp.arange(2 * H_C)[:, None])
        qrows = jnp.where(sel[None, :, None, :], q[:, None, :, :], jnp.zeros((), q.dtype))
        qrows = qrows.reshape(Bs, 2 * H_C * Ts, D_QKC)
        kn = jnp.pad(jnp.transpose(kf, (0, 2, 1)), ((0, 0), (0, 0), (0, page_size - Ts)))
        vn = jnp.pad(vf, ((0, 0), (0, page_size - Ts), (0, 0))).reshape(Bs, page_size * H_C, 2 * HD_C)
        tok = jnp.arange(2 * H_C * Ts) % Ts
        col = jnp.arange(page_size)
        bias = jnp.where(col[None, :] <= tok[:, None], 0.0, -jnp.inf).astype(F32)
        return _paged_attn(page_table, lq, qrows, ck, cv, kn, vn, bias, z, gn_d, lam_init, P=8)

    outs_s = _forward(x_sample, past, state_mlstm_C[0], state_mlstm_n[0], state_mlstm_m[0],
                      state_mlstm_conv[0], state_ret_S[0], attend_sample, W)

    return (outs_p[0], outs_s[0]) + tuple(outs_p[1:]) + tuple(outs_s[1:])
```

```python
import functools
import math

import jax
import jax.numpy as jnp
from jax import lax
from jax.experimental import pallas as pl
from jax.experimental.pallas import tpu as pltpu

F32 = jnp.float32
BF16 = jnp.bfloat16

D_MODEL = 1024
H_M, DK_M, DV_M = 4, 256, 256
D_QKM = H_M * DK_M
D_M = H_M * DV_M
CONV_W = 4
H_R, DK_R, DV_R = 4, 256, 256
D_QKR = H_R * DK_R
D_R = H_R * DV_R
RET_THETA = 10000.0
H_C, HD_C = 8, 64
D_QKC = 2 * H_C * HD_C
D_C = H_C * 2 * HD_C
ROPE_THETA = 500000.0
ROPE_DIM = HD_C // 4
CHUNK = 256
EPS = 1e-6
PROJ_MAIN_A = 2 * D_QKM + 2 * D_M + 2 * D_QKR + 2 * D_R
LANES = 128
LOG2E = math.log2(math.e)
VMEM_LIMIT = 52 * 1024 * 1024

OFF_QKM, OFF_VM, OFF_ZM = 0, 2 * D_QKM, 2 * D_QKM + D_M
OFF_QR = 2 * D_QKM + 2 * D_M
OFF_KR, OFF_VR, OFF_ZR = OFF_QR + D_QKR, OFF_QR + 2 * D_QKR, OFF_QR + 2 * D_QKR + D_R


def _params(sem):
    return pltpu.CompilerParams(dimension_semantics=sem, vmem_limit_bytes=VMEM_LIMIT)


def _rms(x, g):
    return x * lax.rsqrt(jnp.mean(x * x, axis=-1, keepdims=True) + EPS) * g


def _silu(x):
    return x * jax.nn.sigmoid(x)


def _dot(a, b):
    return jnp.dot(a, b, preferred_element_type=F32)


def _dot_nt(a, b):
    return lax.dot_general(a, b, (((1,), (1,)), ((), ())), preferred_element_type=F32)


def _dot_tn(a, b):
    return lax.dot_general(a, b, (((0,), (0,)), ((), ())), preferred_element_type=F32)


def _in_proj_a_kernel(x_ref, g_ref, w_ref, wg_ref, o_ref, og_ref, xn_ref):
    @pl.when(pl.program_id(1) == 0)
    def _():
        xn = _rms(x_ref[...], g_ref[...]).astype(BF16)
        xn_ref[...] = xn
        og_ref[...] = _dot(xn, wg_ref[...])

    o_ref[...] = _dot(xn_ref[...], w_ref[...]).astype(o_ref.dtype)


def _in_proj_a(x, g, w_main, w_gate, out_dtype):
    M, K = x.shape
    N = w_main.shape[1]
    tm = min(M, 1024)
    tn = 2048
    return pl.pallas_call(
        _in_proj_a_kernel,
        grid=(M // tm, N // tn),
        in_specs=[
            pl.BlockSpec((tm, K), lambda i, j: (i, 0)),
            pl.BlockSpec((1, K), lambda i, j: (0, 0)),
            pl.BlockSpec((K, tn), lambda i, j: (0, j)),
            pl.BlockSpec((K, LANES), lambda i, j: (0, 0)),
        ],
        out_specs=[
            pl.BlockSpec((tm, tn), lambda i, j: (i, j)),
            pl.BlockSpec((tm, LANES), lambda i, j: (i, 0)),
        ],
        out_shape=[
            jax.ShapeDtypeStruct((M, N), out_dtype),
            jax.ShapeDtypeStruct((M, LANES), F32),
        ],
        scratch_shapes=[pltpu.VMEM((tm, K), BF16)],
        compiler_params=_params(("parallel", "arbitrary")),
        name="in_proj_even",
    )(x, g, w_main, w_gate)


def _even_mixer_kernel(p_ref, g_ref, cos_ref, sin_ref, gb_ref, cw_ref, cb_ref, gnm_ref, gnr_ref,
                       dec_ref, rtab_ref, cdec_ref, C0_ref, n0_ref, m0_ref, cv0_ref, S0_ref,
                       h_ref, C_ref, n_ref, m_ref, cv_ref, S_ref, xbuf, qkc, qkr, *, TB, L):
    t = pl.program_id(1)
    HIST = 8

    @pl.when(t == 0)
    def _():
        C_ref[...] = C0_ref[...]
        n_ref[...] = n0_ref[...]
        m_ref[...] = m0_ref[...]
        S_ref[...] = S0_ref[...]
        xbuf[HIST - (CONV_W - 1):HIST, :] = cv0_ref[0]

    mxu_shift = p_ref.dtype == BF16 and TB % 16 == 0
    if mxu_shift:
        ti = lax.broadcasted_iota(jnp.int32, (TB, TB), 0)
        si = lax.broadcasted_iota(jnp.int32, (TB, TB), 1)
        shifts = [jnp.where(ti - si == CONV_W - 1 - d, 1.0, 0.0).astype(BF16) for d in range(CONV_W)]
    else:
        @pl.when(t > 0)
        def _():
            xbuf[0:HIST, :] = xbuf[TB:TB + HIST, :]

    for cg in range(2 * D_QKM // 256):
        cs = slice(cg * 256, (cg + 1) * 256)
        scale = DK_M ** -0.5 if cg < D_QKM // 256 else 1.0
        if mxu_shift:
            xb = p_ref[0, :, cs]
            taps = [_dot(sh, xb) for sh in shifts]
            acc = cb_ref[:, cs] + taps[0] * cw_ref[0:1, cs]
            for d in range(1, CONV_W):
                acc = acc + taps[d] * cw_ref[d:d + 1, cs]
            qkc[0:TB, cs] = _silu(acc) * scale
            xbuf[HIST:2 * HIST, cs] = taps[CONV_W - 1][0:HIST]
            head = cb_ref[:, cs] + xbuf[HIST - 3:2 * HIST - 3, cs] * cw_ref[0:1, cs]
            for d in range(1, CONV_W):
                head = head + xbuf[HIST - 3 + d:2 * HIST - 3 + d, cs] * cw_ref[d:d + 1, cs]
            qkc[0:HIST, cs] = _silu(head) * scale
            xbuf[0:HIST, cs] = taps[CONV_W - 1][TB - HIST:TB]
        else:
            x = p_ref[0, :, cs].astype(F32)
            xbuf[HIST:HIST + TB, cs] = x
            acc = cb_ref[:, cs] + xbuf[HIST - 3:HIST - 3 + TB, cs] * cw_ref[0:1, cs]
            acc = acc + xbuf[HIST - 2:HIST - 2 + TB, cs] * cw_ref[1:2, cs]
            acc = acc + xbuf[HIST - 1:HIST - 1 + TB, cs] * cw_ref[2:3, cs]
            acc = acc + x * cw_ref[3:4, cs]
            qkc[0:TB, cs] = _silu(acc) * scale
    if mxu_shift:
        cv_ref[0] = xbuf[HIST - (CONV_W - 1):HIST, :]
    else:
        cv_ref[0] = xbuf[HIST + TB - (CONV_W - 1):HIST + TB, :]

    cos = cos_ref[...]
    sin = sin_ref[...]
    half = DK_R // 2
    for which, off, scale in ((0, OFF_QR, 1.0), (1, OFF_KR, DK_R ** -0.5)):
        for h in range(H_R):
            x1 = p_ref[0, :, off + h * DK_R:off + h * DK_R + half].astype(F32)
            x2 = p_ref[0, :, off + h * DK_R + half:off + (h + 1) * DK_R].astype(F32)
            base = which * D_QKR + h * DK_R
            qkr[0:TB, base:base + half] = (x1 * cos - x2 * sin) * scale
            qkr[0:TB, base + half:base + DK_R] = (x1 * sin + x2 * cos) * scale

    def chunk(rows):
        gates = g_ref[0, rows, :] + gb_ref[...]
        ti = lax.broadcasted_iota(jnp.int32, (L, L), 0)
        si = lax.broadcasted_iota(jnp.int32, (L, L), 1)
        eye = ti == si
        low = si <= ti
        upp = ti <= si
        for h in range(H_M):
            ig = gates[:, h:h + 1]
            lf = jax.nn.log_sigmoid(gates[:, H_M + h:H_M + h + 1])
            lf_r = jnp.sum(jnp.where(eye, lf, 0.0), axis=0, keepdims=True)
            i_r = jnp.sum(jnp.where(eye, ig, 0.0), axis=0, keepdims=True)
            b_c = jnp.sum(jnp.where(low, lf_r, 0.0), axis=1, keepdims=True)
            b_r = jnp.sum(jnp.where(upp, lf, 0.0), axis=0, keepdims=True)
            m_prev = m_ref[0, h:h + 1, 0:1]
            inter = b_c + m_prev
            Dm = jnp.where(low, b_c - b_r + i_r, -jnp.inf)
            m_t = jnp.maximum(inter, jnp.max(Dm, axis=1, keepdims=True))
            w = jnp.exp(Dm - m_t)
            s_int = jnp.exp(inter - m_t)
            q = qkc[rows, h * DK_M:(h + 1) * DK_M]
            k = qkc[rows, D_QKM + h * DK_M:D_QKM + (h + 1) * DK_M]
            vb = p_ref[0, rows, OFF_VM + h * DV_M:OFF_VM + (h + 1) * DV_M].astype(BF16)
            qb = q.astype(BF16)
            qk = _dot_nt(qb, k.astype(BF16)) * w
            C = C_ref[0, h]
            nvec = n_ref[0, h:h + 1, :]
            num = _dot(qk.astype(BF16), vb) + s_int * _dot(qb, C.astype(BF16))
            den = (jnp.sum(qk, axis=1, keepdims=True)
                   + s_int * jnp.sum(q * nvec, axis=1, keepdims=True))
            hh = num / jnp.maximum(jnp.abs(den), jnp.exp(-m_t))
            y = _rms(hh, gnm_ref[:, h * DV_M:(h + 1) * DV_M])
            z = p_ref[0, rows, OFF_ZM + h * DV_M:OFF_ZM + (h + 1) * DV_M].astype(F32)
            h_ref[0, rows, h * DV_M:(h + 1) * DV_M] = (y * _silu(z)).astype(h_ref.dtype)
            bL = b_c[L - 1:L, :]
            dec = bL - b_c + ig
            m_new = jnp.maximum(bL + m_prev, jnp.max(dec, axis=0, keepdims=True))
            ws = jnp.exp(dec - m_new)
            sc = jnp.exp(bL + m_prev - m_new)
            kw = k * ws
            C_ref[0, h] = sc * C + _dot_tn(kw.astype(BF16), vb)
            n_ref[0, h:h + 1, :] = sc * nvec + jnp.sum(kw, axis=0, keepdims=True)
            m_ref[0, h:h + 1, :] = jnp.broadcast_to(m_new, (1, LANES))
        for h in range(H_R):
            q = qkr[rows, h * DK_R:(h + 1) * DK_R]
            k = qkr[rows, D_QKR + h * DK_R:D_QKR + (h + 1) * DK_R]
            vb = p_ref[0, rows, OFF_VR + h * DV_R:OFF_VR + (h + 1) * DV_R].astype(BF16)
            qk = _dot_nt(q.astype(BF16), k.astype(BF16)) * dec_ref[h]
            S = S_ref[0, h]
            q_in = rtab_ref[:, h:h + 1]
            k_out = rtab_ref[:, H_R + h:H_R + h + 1]
            o = _dot(qk.astype(BF16), vb) + _dot((q * q_in).astype(BF16), S.astype(BF16))
            S_ref[0, h] = cdec_ref[:, h:h + 1] * S + _dot_tn((k * k_out).astype(BF16), vb)
            y = _rms(o, gnr_ref[:, h * DV_R:(h + 1) * DV_R])
            z = p_ref[0, rows, OFF_ZR + h * DV_R:OFF_ZR + (h + 1) * DV_R].astype(F32)
            h_ref[0, rows, D_M + h * DV_R:D_M + (h + 1) * DV_R] = (y * _silu(z)).astype(h_ref.dtype)

    n_chunks = TB // L
    if n_chunks == 1:
        chunk(slice(0, L))
    else:
        def body(c, carry):
            chunk(pl.ds(pl.multiple_of(c * L, L), L))
            return carry
        lax.fori_loop(0, n_chunks, body, 0)


def _even_mixer(p, gates, cos_r, sin_r, consts, C0, n0, m0, cv0, S0, TB, L):
    B, T, _ = p.shape
    gbias, cw, cb, gnm, gnr, decay, rtab, cdec = consts
    nt = T // TB
    tb8 = -(-TB // 8) * 8
    full = lambda shape: pl.BlockSpec(shape, lambda b, t: (0,) * len(shape))
    per_b = lambda shape: pl.BlockSpec((1,) + shape, lambda b, t: (b,) + (0,) * len(shape))
    kern = functools.partial(_even_mixer_kernel, TB=TB, L=L)
    return pl.pallas_call(
        kern,
        grid=(B, nt),
        in_specs=[
            pl.BlockSpec((1, TB, PROJ_MAIN_A), lambda b, t: (b, t, 0)),
            pl.BlockSpec((1, TB, LANES), lambda b, t: (b, t, 0)),
            pl.BlockSpec((TB, LANES), lambda b, t: (t, 0)),
            pl.BlockSpec((TB, LANES), lambda b, t: (t, 0)),
            full((1, LANES)), full((CONV_W, 2 * D_QKM)), full((1, 2 * D_QKM)),
            full((1, D_M)), full((1, D_R)), full((H_R, L, L)), full((L, LANES)), full((1, LANES)),
            per_b((H_M, DK_M, DV_M)), per_b((H_M, DK_M)), per_b((H_M, LANES)),
            per_b((CONV_W - 1, 2 * D_QKM)), per_b((H_R, DK_R, DV_R)),
        ],
        out_specs=[
            pl.BlockSpec((1, TB, D_M + D_R), lambda b, t: (b, t, 0)),
            per_b((H_M, DK_M, DV_M)), per_b((H_M, DK_M)), per_b((H_M, LANES)),
            per_b((CONV_W - 1, 2 * D_QKM)), per_b((H_R, DK_R, DV_R)),
        ],
        out_shape=[
            jax.ShapeDtypeStruct((B, T, D_M + D_R), BF16),
            jax.ShapeDtypeStruct((B, H_M, DK_M, DV_M), F32),
            jax.ShapeDtypeStruct((B, H_M, DK_M), F32),
            jax.ShapeDtypeStruct((B, H_M, LANES), F32),
            jax.ShapeDtypeStruct((B, CONV_W - 1, 2 * D_QKM), F32),
            jax.ShapeDtypeStruct((B, H_R, DK_R, DV_R), F32),
        ],
        scratch_shapes=[
            pltpu.VMEM((tb8 + 8, 2 * D_QKM), F32),
            pltpu.VMEM((tb8, 2 * D_QKM), F32),
            pltpu.VMEM((tb8, 2 * D_QKR), F32),
        ],
        compiler_params=_params(("parallel", "arbitrary")),
        name="even_mixer",
    )(p, gates, cos_r, sin_r, gbias, cw, cb, gnm, gnr, decay, rtab, cdec, C0, n0, m0, cv0, S0)


def _out_proj_kernel(h_ref, w_ref, x_ref, g_ref, y_ref):
    o = _dot(h_ref[...], w_ref[...])
    y_ref[...] = x_ref[...] + _rms(o, g_ref[...])


def _out_proj(h, w, x, g):
    M, K = h.shape
    N = w.shape[1]
    tm = min(M, 512)
    return pl.pallas_call(
        _out_proj_kernel,
        grid=(M // tm,),
        in_specs=[
            pl.BlockSpec((tm, K), lambda i: (i, 0)),
            pl.BlockSpec((K, N), lambda i: (0, 0)),
            pl.BlockSpec((tm, N), lambda i: (i, 0)),
            pl.BlockSpec((1, N), lambda i: (0, 0)),
        ],
        out_specs=pl.BlockSpec((tm, N), lambda i: (i, 0)),
        out_shape=jax.ShapeDtypeStruct((M, N), F32),
        compiler_params=_params(("parallel",)),
        name="out_proj",
    )(h, w, x, g)


def _in_proj_c_kernel(x_ref, g_ref, w_ref, ra_ref, rb_ref, rc_ref, *rest, k_transposed):
    if k_transposed:
        wkt_ref, cos_t_ref, sin_t_ref, q_ref, kf_ref, kb_ref, vf_ref, vb_ref, z_ref = rest
    else:
        q_ref, kf_ref, kb_ref, vf_ref, vb_ref, z_ref = rest
    xn = _rms(x_ref[...], g_ref[...]).astype(BF16)
    ra = ra_ref[...]
    rb = rb_ref[...]
    rc = rc_ref[...]
    half = ROPE_DIM // 2

    def rope(y):
        outs = []
        for gi in range(y.shape[1] // LANES):
            yg = y[:, gi * LANES:(gi + 1) * LANES]
            outs.append(yg * ra + pltpu.roll(yg, half, 1) * rb + pltpu.roll(yg, LANES - half, 1) * rc)
        return jnp.concatenate(outs, axis=1)

    q = rope(_dot(xn, w_ref[:, 0:D_QKC]))
    q_ref[...] = (q * (HD_C ** -0.5 * LOG2E)).astype(BF16)
    k = rope(_dot(xn, w_ref[:, D_QKC:2 * D_QKC]))
    kb_ref[...] = k.astype(BF16)
    if k_transposed:
        kt = _dot_nt(wkt_ref[...], xn)
        cos_t = cos_t_ref[...]
        sin_t = sin_t_ref[...]
        pieces = []
        for h in range(2 * H_C):
            x1 = kt[h * HD_C:h * HD_C + half]
            x2 = kt[h * HD_C + half:h * HD_C + ROPE_DIM]
            pieces += [x1 * cos_t - x2 * sin_t, x1 * sin_t + x2 * cos_t, kt[h * HD_C + ROPE_DIM:(h + 1) * HD_C]]
        kf_ref[0] = jnp.concatenate(pieces, axis=0)
    else:
        kf_ref[...] = k
    v = _dot(xn, w_ref[:, 2 * D_QKC:2 * D_QKC + D_C])
    vf_ref[...] = v
    vb_ref[...] = v.astype(BF16)
    z_ref[...] = _dot(xn, w_ref[:, 2 * D_QKC + D_C:]).astype(BF16)


def _in_proj_c(x, g, w, ra, rb, rc, tm, kt_tables=None):
    M, K = x.shape
    n_rt = ra.shape[0] // tm
    row = lambda n: pl.BlockSpec((tm, n), lambda i: (i, 0))
    tab = pl.BlockSpec((tm, LANES), lambda i: (i % n_rt, 0))
    in_specs = [row(K), pl.BlockSpec((1, K), lambda i: (0, 0)), pl.BlockSpec(w.shape, lambda i: (0, 0)),
                tab, tab, tab]
    args = [x, g, w, ra, rb, rc]
    kf_spec, kf_shape = row(D_QKC), jax.ShapeDtypeStruct((M, D_QKC), F32)
    if kt_tables is not None:
        wkt, cos_t, sin_t, T = kt_tables
        tab_t = pl.BlockSpec((cos_t.shape[0], tm), lambda i: (0, i % n_rt))
        in_specs += [pl.BlockSpec(wkt.shape, lambda i: (0, 0)), tab_t, tab_t]
        args += [wkt, cos_t, sin_t]
        kf_spec = pl.BlockSpec((1, D_QKC, tm), lambda i: (i // n_rt, 0, i % n_rt))
        kf_shape = jax.ShapeDtypeStruct((M // T, D_QKC, T), F32)
    return pl.pallas_call(
        functools.partial(_in_proj_c_kernel, k_transposed=kt_tables is not None),
        grid=(M // tm,),
        in_specs=in_specs,
        out_specs=[row(D_QKC), kf_spec, row(D_QKC), row(D_C), row(D_C), row(D_C)],
        out_shape=[
            jax.ShapeDtypeStruct((M, D_QKC), BF16),
            kf_shape,
            jax.ShapeDtypeStruct((M, D_QKC), BF16),
            jax.ShapeDtypeStruct((M, D_C), F32),
            jax.ShapeDtypeStruct((M, D_C), BF16),
            jax.ShapeDtypeStruct((M, D_C), BF16),
        ],
        compiler_params=_params(("parallel",)),
        name="in_proj_odd",
    )(*args)


def _lambda_value(lq_ref, lam_init):
    lf = lq_ref[...]
    l1 = jnp.sum(lf[0:1] * lf[1:2], axis=1, keepdims=True)
    l2 = jnp.sum(lf[2:3] * lf[3:4], axis=1, keepdims=True)
    return jnp.exp(l1) - jnp.exp(l2) + lam_init


def _diff_attn_kernel(it_ref, jt_ref, lq_ref, q_ref, k_ref, v_ref, z_ref, gn_ref, bias_ref, o_ref,
                      vt_sc, q2_sc, s_sc, m_sc, l_sc, acc_sc, *, tq, n_steps, lam_init, G):
    nb = v_ref.shape[1] // tq
    lane = lax.broadcasted_iota(jnp.int32, (tq, LANES), 1)

    def blk(idx):
        return pl.ds(pl.multiple_of(idx * tq, tq), tq)

    def head(g):
        return slice(g * LANES, (g + 1) * LANES)

    for g in range(G):
        for jb in range(nb):
            rows = slice(jb * tq, (jb + 1) * tq)
            vt_sc[g, jb] = v_ref[0, rows, head(g)].astype(F32).T.astype(BF16)
            q = q_ref[0, rows, head(g)]
            zero = jnp.zeros_like(q)
            q2_sc[g, jb, 0:tq, :] = jnp.where(lane < HD_C, q, zero)
            q2_sc[g, jb, tq:2 * tq, :] = jnp.where(lane >= HD_C, q, zero)

    def scores(t, slot, first=False):
        for g in range(G):
            if first:
                s_sc[slot, g] = _dot_nt(k_ref[0, 0:tq, head(g)], q2_sc[g, 0])
            else:
                s_sc[slot, g] = _dot_nt(k_ref[0, blk(jt_ref[t]), head(g)], q2_sc[g, it_ref[t]])

    scores(0, 0, first=True)

    def step(t, slot):
        i = it_ref[t]
        j = jt_ref[t]

        @pl.when(j == 0)
        def _():
            m_sc[...] = jnp.full(m_sc.shape, -jnp.inf, F32)
            l_sc[...] = jnp.zeros(l_sc.shape, F32)
            acc_sc[...] = jnp.zeros(acc_sc.shape, F32)

        scores(t + 1, 1 - slot)
        bias = bias_ref[(j == i).astype(jnp.int32)]
        for g in range(G):
            s = s_sc[slot, g] + bias
            m_prev = m_sc[g]
            m_new = jnp.maximum(m_prev, jnp.max(s, axis=0, keepdims=True))
            p = jnp.exp2(s - m_new)
            alpha = jnp.exp2(m_prev - m_new)
            l_sc[g] = alpha * l_sc[g] + jnp.sum(p, axis=0, keepdims=True)
            acc_sc[g] = alpha * acc_sc[g] + _dot(vt_sc[g, j], p.astype(BF16))
            m_sc[g] = m_new

        @pl.when(j == i)
        def _():
            lam = _lambda_value(lq_ref, lam_init)
            for g in range(G):
                out_t = acc_sc[g] / l_sc[g]
                a_t = out_t[:, 0:tq] - lam * out_t[:, tq:2 * tq]
                a_t = a_t * lax.rsqrt(jnp.mean(a_t * a_t, axis=0, keepdims=True) + EPS)
                y = a_t.T * gn_ref[g] * (1.0 - lam_init)
                z = z_ref[0, blk(i), head(g)].astype(F32)
                o_ref[0, blk(i), head(g)] = (y * _silu(z)).astype(o_ref.dtype)

    def body(u, carry):
        step(2 * u, 0)
        step(2 * u + 1, 1)
        return carry

    assert n_steps % 2 == 0
    lax.fori_loop(0, n_steps // 2, body, 0)


def _diff_attn_prompt(lq, q, k, v, z, gn, lam_init, tq, G):
    B, T, _ = q.shape
    nq = T // tq
    pairs = [(i, j) for i in range(nq) for j in range(i + 1)]
    n_steps = len(pairs)
    pairs.append((0, 0))
    i_tab = jnp.asarray([p[0] for p in pairs], jnp.int32)
    j_tab = jnp.asarray([p[1] for p in pairs], jnp.int32)
    qpos = jnp.arange(2 * tq) % tq
    causal = jnp.where(jnp.arange(tq)[:, None] <= qpos[None, :], 0.0, -jnp.inf).astype(F32)
    bias = jnp.stack([jnp.zeros_like(causal), causal])
    kern = functools.partial(_diff_attn_kernel, tq=tq, n_steps=n_steps, lam_init=lam_init, G=G)
    seq = pl.BlockSpec((1, T, G * LANES), lambda b, h, it, jt: (b, 0, h))
    grid_spec = pltpu.PrefetchScalarGridSpec(
        num_scalar_prefetch=2,
        grid=(B, H_C // G),
        in_specs=[
            pl.BlockSpec(lq.shape, lambda b, h, it, jt: (0, 0)),
            seq, seq, seq, seq,
            pl.BlockSpec((G, 1, LANES), lambda b, h, it, jt: (h, 0, 0)),
            pl.BlockSpec(bias.shape, lambda b, h, it, jt: (0, 0, 0)),
        ],
        out_specs=seq,
        scratch_shapes=[
            pltpu.VMEM((G, nq, LANES, tq), BF16),
            pltpu.VMEM((G, nq, 2 * tq, LANES), BF16),
            pltpu.VMEM((2, G, tq, 2 * tq), F32),
            pltpu.VMEM((G, 1, 2 * tq), F32),
            pltpu.VMEM((G, 1, 2 * tq), F32),
            pltpu.VMEM((G, LANES, 2 * tq), F32),
        ],
    )
    return pl.pallas_call(
        kern,
        grid_spec=grid_spec,
        out_shape=jax.ShapeDtypeStruct((B, T, D_C), BF16),
        compiler_params=_params(("parallel", "parallel")),
        name="diff_attn_prompt",
    )(i_tab, j_tab, lq, q, k, v, z, gn, bias)


def _paged_attn_kernel(pt_ref, lq_ref, qr_ref, *rest, P, lam_init, Tn):
    k_refs = rest[:P]
    v_refs = rest[P:2 * P]
    kn_ref, vn_ref, bias_ref, z_ref, gn_ref, o_ref, m_sc, l_sc, acc_sc = rest[2 * P:]
    j = pl.program_id(1)
    GR = 2 * Tn

    @pl.when(j == 0)
    def _():
        m_sc[...] = jnp.full(m_sc.shape, -jnp.inf, F32)
        l_sc[...] = jnp.zeros(l_sc.shape, F32)
        acc_sc[...] = jnp.zeros(acc_sc.shape, F32)

    qr = qr_ref[0]
    n_tok = k_refs[0].shape[2]

    def pages(kt_refs, vp_refs, bias):
        tiles = [_dot(qr, kt_ref[0].astype(BF16)) for kt_ref in kt_refs]
        if bias is not None:
            tiles = [x + bias for x in tiles]
        m_prev = m_sc[...]
        m_new = jnp.maximum(m_prev, jnp.max(functools.reduce(jnp.maximum, tiles), axis=1, keepdims=True))
        alpha = jnp.exp2(m_prev - m_new)
        ps = [jnp.exp2(x - m_new) for x in tiles]
        l_sc[...] = alpha * l_sc[...] + jnp.sum(functools.reduce(jnp.add, ps), axis=1, keepdims=True)
        m_sc[...] = m_new
        for hp in range(H_C):
            rs = slice(hp * GR, (hp + 1) * GR)
            p_h = jnp.concatenate([x[rs, :].astype(BF16) for x in ps], axis=1)
            v_h = jnp.concatenate(
                [vp_ref[0, pl.ds(hp, n_tok, stride=H_C), :].astype(BF16) for vp_ref in vp_refs], axis=0)
            acc_sc[rs, :] = alpha[rs, :] * acc_sc[rs, :] + _dot(p_h, v_h)

    pages(k_refs, v_refs, None)

    @pl.when(j == pl.num_programs(1) - 1)
    def _():
        pages([kn_ref], [vn_ref], bias_ref[...])
        out = acc_sc[...] / l_sc[...]
        lam = _lambda_value(lq_ref, lam_init)
        for hp in range(H_C):
            a = out[hp * GR:hp * GR + Tn] - lam * out[hp * GR + Tn:(hp + 1) * GR]
            y = _rms(a, gn_ref[hp:hp + 1, :]) * (1.0 - lam_init)
            z = z_ref[0, :, hp * LANES:(hp + 1) * LANES].astype(F32)
            o_ref[0, :, hp * LANES:(hp + 1) * LANES] = (y * _silu(z)).astype(o_ref.dtype)


def _paged_attn(page_table, lq, qrows, ck, cv, kn, vn, bias, z, gn, lam_init, P):
    B, R, _ = qrows.shape
    n_pages = page_table.shape[1]
    Tn = z.shape[1]
    kern = functools.partial(_paged_attn_kernel, P=P, lam_init=lam_init, Tn=Tn)

    def page_spec(arr, pi):
        return pl.BlockSpec((1,) + arr.shape[1:], lambda b, j, pt: (pt[b, j * P + pi], 0, 0))

    per_b = lambda shape: pl.BlockSpec((1,) + shape, lambda b, j, pt: (b,) + (0,) * len(shape))
    full = lambda shape: pl.BlockSpec(shape, lambda b, j, pt: (0,) * len(shape))
    grid_spec = pltpu.PrefetchScalarGridSpec(
        num_scalar_prefetch=1,
        grid=(B, n_pages // P),
        in_specs=[full(lq.shape), per_b(qrows.shape[1:])]
        + [page_spec(ck, pi) for pi in range(P)] + [page_spec(cv, pi) for pi in range(P)]
        + [per_b(kn.shape[1:]), per_b(vn.shape[1:]), full(bias.shape), per_b(z.shape[1:]), full(gn.shape)],
        out_specs=per_b((Tn, D_C)),
        scratch_shapes=[pltpu.VMEM((R, LANES), F32)] * 3,
    )
    return pl.pallas_call(
        kern,
        grid_spec=grid_spec,
        out_shape=jax.ShapeDtypeStruct((B, Tn, D_C), BF16),
        compiler_params=_params(("parallel", "arbitrary")),
        name="paged_diff_attn",
    )(page_table, lq, qrows, *([ck] * P), *([cv] * P), kn, vn, bias, z, gn)


def _rope_angles(pos, rot_dim, theta):
    half = rot_dim // 2
    inv = jnp.power(theta, -jnp.arange(half, dtype=F32) / half)
    return pos.astype(F32)[:, None] * inv[None, :]


def _retention_tables(L):
    lg = jnp.log(1.0 - jnp.power(2.0, -5.0 - jnp.arange(H_R, dtype=F32)))
    j = jnp.arange(L, dtype=F32)
    rel = j[:, None] - j[None, :]
    decay = jnp.where((rel >= 0)[None], jnp.exp(rel[None] * lg[:, None, None]), 0.0)
    q_in = jnp.exp((j + 1.0)[:, None] * lg)
    k_out = jnp.exp((L - 1.0 - j)[:, None] * lg)
    rtab = jnp.zeros((L, LANES), F32).at[:, 0:H_R].set(q_in).at[:, H_R:2 * H_R].set(k_out)
    cdec = jnp.zeros((1, LANES), F32).at[0, 0:H_R].set(jnp.exp(L * lg))
    return decay, rtab, cdec


def _partial_rope_tables(pos):
    ang = _rope_angles(pos, ROPE_DIM, ROPE_THETA)
    half = ROPE_DIM // 2
    cos, sin = jnp.cos(ang), jnp.sin(ang)
    T = pos.shape[0]
    ra = jnp.ones((T, HD_C), F32).at[:, 0:half].set(cos).at[:, half:ROPE_DIM].set(cos)
    rb = jnp.zeros((T, HD_C), F32).at[:, half:ROPE_DIM].set(sin)
    rc = jnp.zeros((T, HD_C), F32).at[:, 0:half].set(-sin)
    rep = LANES // HD_C
    return jnp.tile(ra, (1, rep)), jnp.tile(rb, (1, rep)), jnp.tile(rc, (1, rep))


def _forward(x, pos0, st_C, st_n, st_m, st_conv, st_S, attend, W):
    B, T, _ = x.shape
    M = B * T
    act_dtype = BF16 if T >= CHUNK else F32
    pos = pos0 + jnp.arange(T)

    x2 = x.reshape(M, D_MODEL)
    p, gates = _in_proj_a(x2, W["norm_pre"][0:1], W["w_main_a"], W["w_gate_a"], act_dtype)
    L = CHUNK if T % CHUNK == 0 else T
    TB = L
    ang = _rope_angles(pos, DK_R, RET_THETA)
    decay, rtab, cdec = _retention_tables(L)
    consts = (W["gbias"], W["conv_w"], W["conv_b"], W["gn_m"], W["gn_r"], decay, rtab, cdec)
    m0 = jnp.broadcast_to(st_m[..., None], st_m.shape + (LANES,))
    hcat, C, n, m, conv, S = _even_mixer(
        p.reshape(B, T, PROJ_MAIN_A), gates.reshape(B, T, LANES), jnp.cos(ang), jnp.sin(ang),
        consts, st_C, st_n, m0, st_conv, st_S, TB, L)
    x2 = _out_proj(hcat.reshape(M, D_M + D_R), W["w_out_a"], x2, W["norm_post"][0:1])

    tm = min(M, 512)
    n_rep = max(1, tm // T)
    ra, rb, rc = _partial_rope_tables(jnp.tile(pos, n_rep))
    k_transposed = T % tm == 0
    kt_tables = None
    if k_transposed:
        ang_c = _rope_angles(pos, ROPE_DIM, ROPE_THETA)
        kt_tables = (W["w_k_t"], jnp.cos(ang_c).T, jnp.sin(ang_c).T, T)
    q, kf, kb, vf, vb, z = _in_proj_c(x2, W["norm_pre"][1:2], W["w_in_c"], ra, rb, rc, tm, kt_tables)
    if k_transposed:
        k_out = jnp.transpose(kf.reshape(B, 2 * H_C, HD_C, T), (0, 3, 1, 2))[None]
    else:
        kf = kf.reshape(B, T, D_QKC)
        k_out = kf.reshape(1, B, T, 2 * H_C, HD_C)
    a = attend(q.reshape(B, T, D_QKC), kf, kb.reshape(B, T, D_QKC),
               vf.reshape(B, T, D_C), vb.reshape(B, T, D_C), z.reshape(B, T, D_C))
    y = _out_proj(a.reshape(M, D_C), W["w_out_c"], x2, W["norm_post"][1:2])
    return (y.reshape(B, T, D_MODEL), C[None], n[None], m[None, :, :, 0], conv[None], S[None],
            k_out, vf.reshape(1, B, T, H_C, 2 * HD_C))


def kernel(x_prompt, x_sample, state_mlstm_C, state_mlstm_n, state_mlstm_m, state_mlstm_conv, state_ret_S, cache_k, cache_v, page_table, norm_pre, norm_post, w_in_a, b_gate_i, b_gate_f, conv_w, conv_b, gn_mlstm, gn_ret, w_out_a, w_in_c, lambda_qk, gn_diff, w_out_c):
    g0 = 2 * D_QKM + 2 * D_M
    wa = w_in_a[0]
    W = {
        "norm_pre": norm_pre, "norm_post": norm_post,
        "w_main_a": jnp.concatenate([wa[:, :g0], wa[:, g0 + 2 * H_M:]], axis=1).astype(BF16),
        "w_gate_a": jnp.pad(wa[:, g0:g0 + 2 * H_M], ((0, 0), (0, LANES - 2 * H_M))).astype(BF16),
        "gbias": jnp.pad(jnp.concatenate([b_gate_i[0], b_gate_f[0]])[None], ((0, 0), (0, LANES - 2 * H_M))),
        "conv_w": conv_w[0], "conv_b": conv_b[0][None],
        "gn_m": gn_mlstm[0].reshape(1, D_M), "gn_r": gn_ret[0].reshape(1, D_R),
        "w_out_a": w_out_a[0].astype(BF16),
        "w_in_c": w_in_c[0].astype(BF16),
        "w_k_t": w_in_c[0][:, D_QKC:2 * D_QKC].T.astype(BF16),
        "w_out_c": w_out_c[0].astype(BF16),
    }
    lam_init = 0.8 - 0.6 * math.exp(-0.3 * 1)
    lq = lambda_qk[0]
    gn_d = gn_diff[0]

    Bp, Tp, _ = x_prompt.shape

    def attend_prompt(q, kf, kb, vf, vb, z):
        return _diff_attn_prompt(lq, q, kb, vb, z, gn_d[:, None, :], lam_init, tq=256, G=2)

    outs_p = _forward(
        x_prompt, 0,
        jnp.zeros((Bp, H_M, DK_M, DV_M), F32), jnp.zeros((Bp, H_M, DK_M), F32),
        jnp.zeros((Bp, H_M), F32), jnp.zeros((Bp, CONV_W - 1, 2 * D_QKM), F32),
        jnp.zeros((Bp, H_R, DK_R, DV_R), F32), attend_prompt, W)

    Bs, Ts, _ = x_sample.shape
    n_pool, page_size = cache_k.shape[1], cache_k.shape[2]
    past = page_table.shape[1] * page_size
    ck = jnp.transpose(cache_k[0], (0, 2, 3, 1)).reshape(n_pool, D_QKC, page_size)
    cv = cache_v[0].reshape(n_pool, page_size * H_C, 2 * HD_C)

    def attend_sample(q, kf, kb, vf, vb, z):
        head_of_lane = jnp.arange(D_QKC) // HD_C
        sel = (head_of_lane[None, :] == jnp.arange(2 * H_C)[:, None])
        qrows = jnp.where(sel[None, :, None, :], q[:, None, :, :], jnp.zeros((), q.dtype))
        qrows = qrows.reshape(Bs, 2 * H_C * Ts, D_QKC)
        kn = jnp.pad(jnp.transpose(kf, (0, 2, 1)), ((0, 0), (0, 0), (0, page_size - Ts)))
        vn = jnp.pad(vf, ((0, 0), (0, page_size - Ts), (0, 0))).reshape(Bs, page_size * H_C, 2 * HD_C)
        tok = jnp.arange(2 * H_C * Ts) % Ts
        col = jnp.arange(page_size)
        bias = jnp.where(col[None, :] <= tok[:, None], 0.0, -jnp.inf).astype(F32)
        return _paged_attn(page_table, lq, qrows, ck, cv, kn, vn, bias, z, gn_d, lam_init, P=8)

    outs_s = _forward(x_sample, past, state_mlstm_C[0], state_mlstm_n[0], state_mlstm_m[0],
                      state_mlstm_conv[0], state_ret_S[0], attend_sample, W)

    return (outs_p[0], outs_s[0]) + tuple(outs_p[1:]) + tuple(outs_s[1:])
```

```python
import functools
import math

import jax
import jax.numpy as jnp
from jax import lax
from jax.experimental import pallas as pl
from jax.experimental.pallas import tpu as pltpu

F32 = jnp.float32
BF16 = jnp.bfloat16

D_MODEL = 1024
H_M, DK_M, DV_M = 4, 256, 256
D_QKM = H_M * DK_M
D_M = H_M * DV_M
CONV_W = 4
H_R, DK_R, DV_R = 4, 256, 256
D_QKR = H_R * DK_R
D_R = H_R * DV_R
RET_THETA = 10000.0
H_C, HD_C = 8, 64
D_QKC = 2 * H_C * HD_C
D_C = H_C * 2 * HD_C
ROPE_THETA = 500000.0
ROPE_DIM = HD_C // 4
CHUNK = 256
EPS = 1e-6
PROJ_MAIN_A = 2 * D_QKM + 2 * D_M + 2 * D_QKR + 2 * D_R
LANES = 128
LOG2E = math.log2(math.e)
NEW_TOKEN_ROWS = 16
VMEM_LIMIT = 52 * 1024 * 1024

OFF_QKM, OFF_VM, OFF_ZM = 0, 2 * D_QKM, 2 * D_QKM + D_M
OFF_QR = 2 * D_QKM + 2 * D_M
OFF_KR, OFF_VR, OFF_ZR = OFF_QR + D_QKR, OFF_QR + 2 * D_QKR, OFF_QR + 2 * D_QKR + D_R


def _params(sem):
    return pltpu.CompilerParams(dimension_semantics=sem, vmem_limit_bytes=VMEM_LIMIT)


def _rms(x, g):
    return x * lax.rsqrt(jnp.mean(x * x, axis=-1, keepdims=True) + EPS) * g


def _silu(x):
    return x * jax.nn.sigmoid(x)


def _dot(a, b):
    return jnp.dot(a, b, preferred_element_type=F32)


def _dot_nt(a, b):
    return lax.dot_general(a, b, (((1,), (1,)), ((), ())), preferred_element_type=F32)


def _dot_tn(a, b):
    return lax.dot_general(a, b, (((0,), (0,)), ((), ())), preferred_element_type=F32)


def _in_proj_a_kernel(x_ref, g_ref, w_ref, wg_ref, o_ref, og_ref, xn_ref):
    @pl.when(pl.program_id(1) == 0)
    def _():
        xn = _rms(x_ref[...], g_ref[...]).astype(BF16)
        xn_ref[...] = xn
        og_ref[...] = _dot(xn, wg_ref[...])

    o_ref[...] = _dot(xn_ref[...], w_ref[...]).astype(o_ref.dtype)


def _in_proj_a(x, g, w_main, w_gate, out_dtype):
    M, K = x.shape
    N = w_main.shape[1]
    tm = min(M, 1024)
    tn = 2048
    return pl.pallas_call(
        _in_proj_a_kernel,
        grid=(M // tm, N // tn),
        in_specs=[
            pl.BlockSpec((tm, K), lambda i, j: (i, 0)),
            pl.BlockSpec((1, K), lambda i, j: (0, 0)),
            pl.BlockSpec((K, tn), lambda i, j: (0, j)),
            pl.BlockSpec((K, LANES), lambda i, j: (0, 0)),
        ],
        out_specs=[
            pl.BlockSpec((tm, tn), lambda i, j: (i, j)),
            pl.BlockSpec((tm, LANES), lambda i, j: (i, 0)),
        ],
        out_shape=[
            jax.ShapeDtypeStruct((M, N), out_dtype),
            jax.ShapeDtypeStruct((M, LANES), F32),
        ],
        scratch_shapes=[pltpu.VMEM((tm, K), BF16)],
        compiler_params=_params(("parallel", "arbitrary")),
        name="in_proj_even",
    )(x, g, w_main, w_gate)


def _even_mixer_kernel(p_ref, g_ref, cos_ref, sin_ref, gb_ref, cw_ref, cb_ref, gnm_ref, gnr_ref,
                       dec_ref, rtab_ref, cdec_ref, C0_ref, n0_ref, m0_ref, cv0_ref, S0_ref,
                       h_ref, C_ref, n_ref, m_ref, cv_ref, S_ref, xbuf, qkc, qkr, *, TB, L):
    t = pl.program_id(1)
    HIST = 8

    @pl.when(t == 0)
    def _():
        C_ref[...] = C0_ref[...]
        n_ref[...] = n0_ref[...]
        m_ref[...] = m0_ref[...]
        S_ref[...] = S0_ref[...]
        xbuf[HIST - (CONV_W - 1):HIST, :] = cv0_ref[0]

    mxu_shift = p_ref.dtype == BF16 and TB % 16 == 0
    if mxu_shift:
        ti = lax.broadcasted_iota(jnp.int32, (TB, TB), 0)
        si = lax.broadcasted_iota(jnp.int32, (TB, TB), 1)
        shifts = [jnp.where(ti - si == CONV_W - 1 - d, 1.0, 0.0).astype(BF16) for d in range(CONV_W)]
    else:
        @pl.when(t > 0)
        def _():
            xbuf[0:HIST, :] = xbuf[TB:TB + HIST, :]

    for cg in range(2 * D_QKM // 256):
        cs = slice(cg * 256, (cg + 1) * 256)
        scale = DK_M ** -0.5 if cg < D_QKM // 256 else 1.0
        if mxu_shift:
            xb = p_ref[0, :, cs]
            taps = [_dot(sh, xb) for sh in shifts]
            acc = cb_ref[:, cs] + taps[0] * cw_ref[0:1, cs]
            for d in range(1, CONV_W):
                acc = acc + taps[d] * cw_ref[d:d + 1, cs]
            qkc[0:TB, cs] = _silu(acc) * scale
            xbuf[HIST:2 * HIST, cs] = taps[CONV_W - 1][0:HIST]
            head = cb_ref[:, cs] + xbuf[HIST - 3:2 * HIST - 3, cs] * cw_ref[0:1, cs]
            for d in range(1, CONV_W):
                head = head + xbuf[HIST - 3 + d:2 * HIST - 3 + d, cs] * cw_ref[d:d + 1, cs]
            qkc[0:HIST, cs] = _silu(head) * scale
            xbuf[0:HIST, cs] = taps[CONV_W - 1][TB - HIST:TB]
        else:
            x = p_ref[0, :, cs].astype(F32)
            xbuf[HIST:HIST + TB, cs] = x
            acc = cb_ref[:, cs] + xbuf[HIST - 3:HIST - 3 + TB, cs] * cw_ref[0:1, cs]
            acc = acc + xbuf[HIST - 2:HIST - 2 + TB, cs] * cw_ref[1:2, cs]
            acc = acc + xbuf[HIST - 1:HIST - 1 + TB, cs] * cw_ref[2:3, cs]
            acc = acc + x * cw_ref[3:4, cs]
            qkc[0:TB, cs] = _silu(acc) * scale
    if mxu_shift:
        cv_ref[0] = xbuf[HIST - (CONV_W - 1):HIST, :]
    else:
        cv_ref[0] = xbuf[HIST + TB - (CONV_W - 1):HIST + TB, :]

    cos = cos_ref[...]
    sin = sin_ref[...]
    half = DK_R // 2
    for which, off, scale in ((0, OFF_QR, 1.0), (1, OFF_KR, DK_R ** -0.5)):
        for h in range(H_R):
            x1 = p_ref[0, :, off + h * DK_R:off + h * DK_R + half].astype(F32)
            x2 = p_ref[0, :, off + h * DK_R + half:off + (h + 1) * DK_R].astype(F32)
            base = which * D_QKR + h * DK_R
            qkr[0:TB, base:base + half] = (x1 * cos - x2 * sin) * scale
            qkr[0:TB, base + half:base + DK_R] = (x1 * sin + x2 * cos) * scale

    def chunk(rows):
        gates = g_ref[0, rows, :] + gb_ref[...]
        ti = lax.broadcasted_iota(jnp.int32, (L, L), 0)
        si = lax.broadcasted_iota(jnp.int32, (L, L), 1)
        eye = ti == si
        low = si <= ti
        upp = ti <= si
        for h in range(H_M):
            ig = gates[:, h:h + 1]
            lf = jax.nn.log_sigmoid(gates[:, H_M + h:H_M + h + 1])
            lf_r = jnp.sum(jnp.where(eye, lf, 0.0), axis=0, keepdims=True)
            i_r = jnp.sum(jnp.where(eye, ig, 0.0), axis=0, keepdims=True)
            b_c = jnp.sum(jnp.where(low, lf_r, 0.0), axis=1, keepdims=True)
            b_r = jnp.sum(jnp.where(upp, lf, 0.0), axis=0, keepdims=True)
            m_prev = m_ref[0, h:h + 1, 0:1]
            inter = b_c + m_prev
            Dm = jnp.where(low, b_c - b_r + i_r, -jnp.inf)
            m_t = jnp.maximum(inter, jnp.max(Dm, axis=1, keepdims=True))
            w = jnp.exp(Dm - m_t)
            s_int = jnp.exp(inter - m_t)
            q = qkc[rows, h * DK_M:(h + 1) * DK_M]
            k = qkc[rows, D_QKM + h * DK_M:D_QKM + (h + 1) * DK_M]
            vb = p_ref[0, rows, OFF_VM + h * DV_M:OFF_VM + (h + 1) * DV_M].astype(BF16)
            qb = q.astype(BF16)
            qk = _dot_nt(qb, k.astype(BF16)) * w
            C = C_ref[0, h]
            nvec = n_ref[0, h:h + 1, :]
            num = _dot(qk.astype(BF16), vb) + s_int * _dot(qb, C.astype(BF16))
            den = (jnp.sum(qk, axis=1, keepdims=True)
                   + s_int * jnp.sum(q * nvec, axis=1, keepdims=True))
            hh = num / jnp.maximum(jnp.abs(den), jnp.exp(-m_t))
            y = _rms(hh, gnm_ref[:, h * DV_M:(h + 1) * DV_M])
            z = p_ref[0, rows, OFF_ZM + h * DV_M:OFF_ZM + (h + 1) * DV_M].astype(F32)
            h_ref[0, rows, h * DV_M:(h + 1) * DV_M] = (y * _silu(z)).astype(h_ref.dtype)
            bL = b_c[L - 1:L, :]
            dec = bL - b_c + ig
            m_new = jnp.maximum(bL + m_prev, jnp.max(dec, axis=0, keepdims=True))
            ws = jnp.exp(dec - m_new)
            sc = jnp.exp(bL + m_prev - m_new)
            kw = k * ws
            C_ref[0, h] = sc * C + _dot_tn(kw.astype(BF16), vb)
            n_ref[0, h:h + 1, :] = sc * nvec + jnp.sum(kw, axis=0, keepdims=True)
            m_ref[0, h:h + 1, :] = jnp.broadcast_to(m_new, (1, LANES))
        for h in range(H_R):
            q = qkr[rows, h * DK_R:(h + 1) * DK_R]
            k = qkr[rows, D_QKR + h * DK_R:D_QKR + (h + 1) * DK_R]
            vb = p_ref[0, rows, OFF_VR + h * DV_R:OFF_VR + (h + 1) * DV_R].astype(BF16)
            qk = _dot_nt(q.astype(BF16), k.astype(BF16)) * dec_ref[h]
            S = S_ref[0, h]
            q_in = rtab_ref[:, h:h + 1]
            k_out = rtab_ref[:, H_R + h:H_R + h + 1]
            o = _dot(qk.astype(BF16), vb) + _dot((q * q_in).astype(BF16), S.astype(BF16))
            S_ref[0, h] = cdec_ref[:, h:h + 1] * S + _dot_tn((k * k_out).astype(BF16), vb)
            y = _rms(o, gnr_ref[:, h * DV_R:(h + 1) * DV_R])
            z = p_ref[0, rows, OFF_ZR + h * DV_R:OFF_ZR + (h + 1) * DV_R].astype(F32)
            h_ref[0, rows, D_M + h * DV_R:D_M + (h + 1) * DV_R] = (y * _silu(z)).astype(h_ref.dtype)

    n_chunks = TB // L
    if n_chunks == 1:
        chunk(slice(0, L))
    else:
        def body(c, carry):
            chunk(pl.ds(pl.multiple_of(c * L, L), L))
            return carry
        lax.fori_loop(0, n_chunks, body, 0)


def _even_mixer(p, gates, cos_r, sin_r, consts, C0, n0, m0, cv0, S0, TB, L):
    B, T, _ = p.shape
    gbias, cw, cb, gnm, gnr, decay, rtab, cdec = consts
    nt = T // TB
    tb8 = -(-TB // 8) * 8
    full = lambda shape: pl.BlockSpec(shape, lambda b, t: (0,) * len(shape))
    per_b = lambda shape: pl.BlockSpec((1,) + shape, lambda b, t: (b,) + (0,) * len(shape))
    kern = functools.partial(_even_mixer_kernel, TB=TB, L=L)
    return pl.pallas_call(
        kern,
        grid=(B, nt),
        in_specs=[
            pl.BlockSpec((1, TB, PROJ_MAIN_A), lambda b, t: (b, t, 0)),
            pl.BlockSpec((1, TB, LANES), lambda b, t: (b, t, 0)),
            pl.BlockSpec((TB, LANES), lambda b, t: (t, 0)),
            pl.BlockSpec((TB, LANES), lambda b, t: (t, 0)),
            full((1, LANES)), full((CONV_W, 2 * D_QKM)), full((1, 2 * D_QKM)),
            full((1, D_M)), full((1, D_R)), full((H_R, L, L)), full((L, LANES)), full((1, LANES)),
            per_b((H_M, DK_M, DV_M)), per_b((H_M, DK_M)), per_b((H_M, LANES)),
            per_b((CONV_W - 1, 2 * D_QKM)), per_b((H_R, DK_R, DV_R)),
        ],
        out_specs=[
            pl.BlockSpec((1, TB, D_M + D_R), lambda b, t: (b, t, 0)),
            per_b((H_M, DK_M, DV_M)), per_b((H_M, DK_M)), per_b((H_M, LANES)),
            per_b((CONV_W - 1, 2 * D_QKM)), per_b((H_R, DK_R, DV_R)),
        ],
        out_shape=[
            jax.ShapeDtypeStruct((B, T, D_M + D_R), BF16),
            jax.ShapeDtypeStruct((B, H_M, DK_M, DV_M), F32),
            jax.ShapeDtypeStruct((B, H_M, DK_M), F32),
            jax.ShapeDtypeStruct((B, H_M, LANES), F32),
            jax.ShapeDtypeStruct((B, CONV_W - 1, 2 * D_QKM), F32),
            jax.ShapeDtypeStruct((B, H_R, DK_R, DV_R), F32),
        ],
        scratch_shapes=[
            pltpu.VMEM((tb8 + 8, 2 * D_QKM), F32),
            pltpu.VMEM((tb8, 2 * D_QKM), F32),
            pltpu.VMEM((tb8, 2 * D_QKR), F32),
        ],
        compiler_params=_params(("parallel", "arbitrary")),
        name="even_mixer",
    )(p, gates, cos_r, sin_r, gbias, cw, cb, gnm, gnr, decay, rtab, cdec, C0, n0, m0, cv0, S0)


def _out_proj_kernel(h_ref, w_ref, x_ref, g_ref, y_ref):
    o = _dot(h_ref[...], w_ref[...])
    y_ref[...] = x_ref[...] + _rms(o, g_ref[...])


def _out_proj(h, w, x, g):
    M, K = h.shape
    N = w.shape[1]
    tm = min(M, 1024)
    return pl.pallas_call(
        _out_proj_kernel,
        grid=(M // tm,),
        in_specs=[
            pl.BlockSpec((tm, K), lambda i: (i, 0)),
            pl.BlockSpec((K, N), lambda i: (0, 0)),
            pl.BlockSpec((tm, N), lambda i: (i, 0)),
            pl.BlockSpec((1, N), lambda i: (0, 0)),
        ],
        out_specs=pl.BlockSpec((tm, N), lambda i: (i, 0)),
        out_shape=jax.ShapeDtypeStruct((M, N), F32),
        compiler_params=_params(("parallel",)),
        name="out_proj",
    )(h, w, x, g)


def _in_proj_c_kernel(x_ref, g_ref, w_ref, ra_ref, rb_ref, rc_ref, *rest, k_transposed):
    if k_transposed:
        wkt_ref, cos_t_ref, sin_t_ref, q_ref, kf_ref, kb_ref, vf_ref, vb_ref, z_ref = rest
    else:
        q_ref, kf_ref, kb_ref, vf_ref, vb_ref, z_ref = rest
    xn = _rms(x_ref[...], g_ref[...]).astype(BF16)
    ra = ra_ref[...]
    rb = rb_ref[...]
    rc = rc_ref[...]
    half = ROPE_DIM // 2

    def rope(y):
        outs = []
        for gi in range(y.shape[1] // LANES):
            yg = y[:, gi * LANES:(gi + 1) * LANES]
            outs.append(yg * ra + pltpu.roll(yg, half, 1) * rb + pltpu.roll(yg, LANES - half, 1) * rc)
        return jnp.concatenate(outs, axis=1)

    q = rope(_dot(xn, w_ref[:, 0:D_QKC]))
    q_ref[...] = (q * (HD_C ** -0.5 * LOG2E)).astype(BF16)
    k = rope(_dot(xn, w_ref[:, D_QKC:2 * D_QKC]))
    kb_ref[...] = k.astype(BF16)
    if k_transposed:
        kt = _dot_nt(wkt_ref[...], xn)
        cos_t = cos_t_ref[...]
        sin_t = sin_t_ref[...]
        pieces = []
        for h in range(2 * H_C):
            x1 = kt[h * HD_C:h * HD_C + half]
            x2 = kt[h * HD_C + half:h * HD_C + ROPE_DIM]
            pieces += [x1 * cos_t - x2 * sin_t, x1 * sin_t + x2 * cos_t, kt[h * HD_C + ROPE_DIM:(h + 1) * HD_C]]
        kf_ref[0] = jnp.concatenate(pieces, axis=0)
    else:
        kf_ref[...] = k
    v = _dot(xn, w_ref[:, 2 * D_QKC:2 * D_QKC + D_C])
    vf_ref[...] = v
    vb_ref[...] = v.astype(BF16)
    z_ref[...] = _dot(xn, w_ref[:, 2 * D_QKC + D_C:]).astype(BF16)


def _in_proj_c(x, g, w, ra, rb, rc, tm, kt_tables=None):
    M, K = x.shape
    n_rt = ra.shape[0] // tm
    row = lambda n: pl.BlockSpec((tm, n), lambda i: (i, 0))
    tab = pl.BlockSpec((tm, LANES), lambda i: (i % n_rt, 0))
    in_specs = [row(K), pl.BlockSpec((1, K), lambda i: (0, 0)), pl.BlockSpec(w.shape, lambda i: (0, 0)),
                tab, tab, tab]
    args = [x, g, w, ra, rb, rc]
    kf_spec, kf_shape = row(D_QKC), jax.ShapeDtypeStruct((M, D_QKC), F32)
    if kt_tables is not None:
        wkt, cos_t, sin_t, T = kt_tables
        tab_t = pl.BlockSpec((cos_t.shape[0], tm), lambda i: (0, i % n_rt))
        in_specs += [pl.BlockSpec(wkt.shape, lambda i: (0, 0)), tab_t, tab_t]
        args += [wkt, cos_t, sin_t]
        kf_spec = pl.BlockSpec((1, D_QKC, tm), lambda i: (i // n_rt, 0, i % n_rt))
        kf_shape = jax.ShapeDtypeStruct((M // T, D_QKC, T), F32)
    return pl.pallas_call(
        functools.partial(_in_proj_c_kernel, k_transposed=kt_tables is not None),
        grid=(M // tm,),
        in_specs=in_specs,
        out_specs=[row(D_QKC), kf_spec, row(D_QKC), row(D_C), row(D_C), row(D_C)],
        out_shape=[
            jax.ShapeDtypeStruct((M, D_QKC), BF16),
            kf_shape,
            jax.ShapeDtypeStruct((M, D_QKC), BF16),
            jax.ShapeDtypeStruct((M, D_C), F32),
            jax.ShapeDtypeStruct((M, D_C), BF16),
            jax.ShapeDtypeStruct((M, D_C), BF16),
        ],
        compiler_params=_params(("parallel",)),
        name="in_proj_odd",
    )(*args)


def _lambda_value(lq_ref, lam_init):
    lf = lq_ref[...]
    l1 = jnp.sum(lf[0:1] * lf[1:2], axis=1, keepdims=True)
    l2 = jnp.sum(lf[2:3] * lf[3:4], axis=1, keepdims=True)
    return jnp.exp(l1) - jnp.exp(l2) + lam_init


def _diff_attn_kernel(it_ref, jt_ref, lq_ref, q_ref, k_ref, v_ref, z_ref, gn_ref, bias_ref, o_ref,
                      vt_sc, q2_sc, s_sc, m_sc, l_sc, acc_sc, *, tq, n_steps, lam_init, G):
    nb = v_ref.shape[1] // tq
    lane = lax.broadcasted_iota(jnp.int32, (tq, LANES), 1)

    def blk(idx):
        return pl.ds(pl.multiple_of(idx * tq, tq), tq)

    def head(g):
        return slice(g * LANES, (g + 1) * LANES)

    for g in range(G):
        for jb in range(nb):
            rows = slice(jb * tq, (jb + 1) * tq)
            vt_sc[g, jb] = v_ref[0, rows, head(g)].astype(F32).T.astype(BF16)
            q = q_ref[0, rows, head(g)]
            zero = jnp.zeros_like(q)
            q2_sc[g, jb, 0:tq, :] = jnp.where(lane < HD_C, q, zero)
            q2_sc[g, jb, tq:2 * tq, :] = jnp.where(lane >= HD_C, q, zero)

    def scores(t, slot, first=False):
        for g in range(G):
            if first:
                s_sc[slot, g] = _dot_nt(k_ref[0, 0:tq, head(g)], q2_sc[g, 0])
            else:
                s_sc[slot, g] = _dot_nt(k_ref[0, blk(jt_ref[t]), head(g)], q2_sc[g, it_ref[t]])

    scores(0, 0, first=True)

    def step(t, slot):
        i = it_ref[t]
        j = jt_ref[t]

        @pl.when(j == 0)
        def _():
            m_sc[...] = jnp.full(m_sc.shape, -jnp.inf, F32)
            l_sc[...] = jnp.zeros(l_sc.shape, F32)
            acc_sc[...] = jnp.zeros(acc_sc.shape, F32)

        scores(t + 1, 1 - slot)
        bias = bias_ref[(j == i).astype(jnp.int32)]
        for g in range(G):
            s = s_sc[slot, g] + bias
            m_prev = m_sc[g]
            m_new = jnp.maximum(m_prev, jnp.max(s, axis=0, keepdims=True))
            p = jnp.exp2(s - m_new)
            alpha = jnp.exp2(m_prev - m_new)
            l_sc[g] = alpha * l_sc[g] + jnp.sum(p, axis=0, keepdims=True)
            acc_sc[g] = alpha * acc_sc[g] + _dot(vt_sc[g, j], p.astype(BF16))
            m_sc[g] = m_new

        @pl.when(j == i)
        def _():
            lam = _lambda_value(lq_ref, lam_init)
            for g in range(G):
                out_t = acc_sc[g] / l_sc[g]
                a_t = out_t[:, 0:tq] - lam * out_t[:, tq:2 * tq]
                a_t = a_t * lax.rsqrt(jnp.mean(a_t * a_t, axis=0, keepdims=True) + EPS)
                y = a_t.T * gn_ref[g] * (1.0 - lam_init)
                z = z_ref[0, blk(i), head(g)].astype(F32)
                o_ref[0, blk(i), head(g)] = (y * _silu(z)).astype(o_ref.dtype)

    def body(u, carry):
        step(2 * u, 0)
        step(2 * u + 1, 1)
        return carry

    assert n_steps % 2 == 0
    lax.fori_loop(0, n_steps // 2, body, 0)


def _diff_attn_prompt(lq, q, k, v, z, gn, lam_init, tq, G):
    B, T, _ = q.shape
    nq = T // tq
    pairs = [(i, j) for i in range(nq) for j in range(i + 1)]
    n_steps = len(pairs)
    pairs.append((0, 0))
    i_tab = jnp.asarray([p[0] for p in pairs], jnp.int32)
    j_tab = jnp.asarray([p[1] for p in pairs], jnp.int32)
    qpos = jnp.arange(2 * tq) % tq
    causal = jnp.where(jnp.arange(tq)[:, None] <= qpos[None, :], 0.0, -jnp.inf).astype(F32)
    bias = jnp.stack([jnp.zeros_like(causal), causal])
    kern = functools.partial(_diff_attn_kernel, tq=tq, n_steps=n_steps, lam_init=lam_init, G=G)
    seq = pl.BlockSpec((1, T, G * LANES), lambda b, h, it, jt: (b, 0, h))
    grid_spec = pltpu.PrefetchScalarGridSpec(
        num_scalar_prefetch=2,
        grid=(B, H_C // G),
        in_specs=[
            pl.BlockSpec(lq.shape, lambda b, h, it, jt: (0, 0)),
            seq, seq, seq, seq,
            pl.BlockSpec((G, 1, LANES), lambda b, h, it, jt: (h, 0, 0)),
            pl.BlockSpec(bias.shape, lambda b, h, it, jt: (0, 0, 0)),
        ],
        out_specs=seq,
        scratch_shapes=[
            pltpu.VMEM((G, nq, LANES, tq), BF16),
            pltpu.VMEM((G, nq, 2 * tq, LANES), BF16),
            pltpu.VMEM((2, G, tq, 2 * tq), F32),
            pltpu.VMEM((G, 1, 2 * tq), F32),
            pltpu.VMEM((G, 1, 2 * tq), F32),
            pltpu.VMEM((G, LANES, 2 * tq), F32),
        ],
    )
    return pl.pallas_call(
        kern,
        grid_spec=grid_spec,
        out_shape=jax.ShapeDtypeStruct((B, T, D_C), BF16),
        compiler_params=_params(("parallel", "parallel")),
        name="diff_attn_prompt",
    )(i_tab, j_tab, lq, q, k, v, z, gn, bias)


def _paged_attn_kernel(pt_ref, lq_ref, qr_ref, *rest, P, lam_init, Tn):
    k_refs = rest[:P]
    v_refs = rest[P:2 * P]
    kn_ref, vn_ref, bias_ref, z_ref, gn_ref, o_ref, m_sc, l_sc, acc_sc = rest[2 * P:]
    j = pl.program_id(1)
    GR = 2 * Tn

    @pl.when(j == 0)
    def _():
        m_sc[...] = jnp.full(m_sc.shape, -jnp.inf, F32)
        l_sc[...] = jnp.zeros(l_sc.shape, F32)
        acc_sc[...] = jnp.zeros(acc_sc.shape, F32)

    qr = qr_ref[0]

    def pages(kt_refs, vp_refs, bias):
        tiles = [_dot(qr, kt_ref[0].astype(BF16)) for kt_ref in kt_refs]
        if bias is not None:
            tiles = [x + bias for x in tiles]
        m_prev = m_sc[...]
        m_new = jnp.maximum(m_prev, jnp.max(functools.reduce(jnp.maximum, tiles), axis=1, keepdims=True))
        alpha = jnp.exp2(m_prev - m_new)
        m_tile = m_new[:, 0:tiles[0].shape[1]]
        ps = [jnp.exp2(x - m_tile) for x in tiles]
        l_sc[...] = alpha * l_sc[...] + jnp.sum(functools.reduce(jnp.add, ps), axis=1, keepdims=True)
        m_sc[...] = m_new
        for hp in range(H_C):
            rs = slice(hp * GR, (hp + 1) * GR)
            p_h = jnp.concatenate([x[rs, :].astype(BF16) for x in ps], axis=1)
            v_h = jnp.concatenate(
                [vp_ref[0, pl.ds(hp, vp_ref.shape[1] // H_C, stride=H_C), :].astype(BF16) for vp_ref in vp_refs],
                axis=0)
            acc_sc[rs, :] = alpha[rs, :] * acc_sc[rs, :] + _dot(p_h, v_h)

    pages(k_refs, v_refs, None)

    @pl.when(j == pl.num_programs(1) - 1)
    def _():
        pages([kn_ref], [vn_ref], bias_ref[...])
        out = acc_sc[...] / l_sc[...]
        lam = _lambda_value(lq_ref, lam_init)
        for hp in range(H_C):
            a = out[hp * GR:hp * GR + Tn] - lam * out[hp * GR + Tn:(hp + 1) * GR]
            y = _rms(a, gn_ref[hp:hp + 1, :]) * (1.0 - lam_init)
            z = z_ref[0, :, hp * LANES:(hp + 1) * LANES].astype(F32)
            o_ref[0, :, hp * LANES:(hp + 1) * LANES] = (y * _silu(z)).astype(o_ref.dtype)


def _paged_attn(page_table, lq, qrows, ck, cv, kn, vn, bias, z, gn, lam_init, P):
    B, R, _ = qrows.shape
    n_pages = page_table.shape[1]
    Tn = z.shape[1]
    kern = functools.partial(_paged_attn_kernel, P=P, lam_init=lam_init, Tn=Tn)

    def page_spec(arr, pi):
        return pl.BlockSpec((1,) + arr.shape[1:], lambda b, j, pt: (pt[b, j * P + pi], 0, 0))

    per_b = lambda shape: pl.BlockSpec((1,) + shape, lambda b, j, pt: (b,) + (0,) * len(shape))
    full = lambda shape: pl.BlockSpec(shape, lambda b, j, pt: (0,) * len(shape))
    grid_spec = pltpu.PrefetchScalarGridSpec(
        num_scalar_prefetch=1,
        grid=(B, n_pages // P),
        in_specs=[full(lq.shape), per_b(qrows.shape[1:])]
        + [page_spec(ck, pi) for pi in range(P)] + [page_spec(cv, pi) for pi in range(P)]
        + [per_b(kn.shape[1:]), per_b(vn.shape[1:]), full(bias.shape), per_b(z.shape[1:]), full(gn.shape)],
        out_specs=per_b((Tn, D_C)),
        scratch_shapes=[pltpu.VMEM((R, LANES), F32)] * 3,
    )
    return pl.pallas_call(
        kern,
        grid_spec=grid_spec,
        out_shape=jax.ShapeDtypeStruct((B, Tn, D_C), BF16),
        compiler_params=_params(("parallel", "arbitrary")),
        name="paged_diff_attn",
    )(page_table, lq, qrows, *([ck] * P), *([cv] * P), kn, vn, bias, z, gn)


def _rope_angles(pos, rot_dim, theta):
    half = rot_dim // 2
    inv = jnp.power(theta, -jnp.arange(half, dtype=F32) / half)
    return pos.astype(F32)[:, None] * inv[None, :]


def _retention_tables(L):
    lg = jnp.log(1.0 - jnp.power(2.0, -5.0 - jnp.arange(H_R, dtype=F32)))
    j = jnp.arange(L, dtype=F32)
    rel = j[:, None] - j[None, :]
    decay = jnp.where((rel >= 0)[None], jnp.exp(rel[None] * lg[:, None, None]), 0.0)
    q_in = jnp.exp((j + 1.0)[:, None] * lg)
    k_out = jnp.exp((L - 1.0 - j)[:, None] * lg)
    rtab = jnp.zeros((L, LANES), F32).at[:, 0:H_R].set(q_in).at[:, H_R:2 * H_R].set(k_out)
    cdec = jnp.zeros((1, LANES), F32).at[0, 0:H_R].set(jnp.exp(L * lg))
    return decay, rtab, cdec


def _partial_rope_tables(pos):
    ang = _rope_angles(pos, ROPE_DIM, ROPE_THETA)
    half = ROPE_DIM // 2
    cos, sin = jnp.cos(ang), jnp.sin(ang)
    T = pos.shape[0]
    ra = jnp.ones((T, HD_C), F32).at[:, 0:half].set(cos).at[:, half:ROPE_DIM].set(cos)
    rb = jnp.zeros((T, HD_C), F32).at[:, half:ROPE_DIM].set(sin)
    rc = jnp.zeros((T, HD_C), F32).at[:, 0:half].set(-sin)
    rep = LANES // HD_C
    return jnp.tile(ra, (1, rep)), jnp.tile(rb, (1, rep)), jnp.tile(rc, (1, rep))


def _forward(x, pos0, st_C, st_n, st_m, st_conv, st_S, attend, W):
    B, T, _ = x.shape
    M = B * T
    act_dtype = BF16 if T >= CHUNK else F32
    pos = pos0 + jnp.arange(T)

    x2 = x.reshape(M, D_MODEL)
    p, gates = _in_proj_a(x2, W["norm_pre"][0:1], W["w_main_a"], W["w_gate_a"], act_dtype)
    L = CHUNK if T % CHUNK == 0 else T
    TB = L
    ang = _rope_angles(pos, DK_R, RET_THETA)
    decay, rtab, cdec = _retention_tables(L)
    consts = (W["gbias"], W["conv_w"], W["conv_b"], W["gn_m"], W["gn_r"], decay, rtab, cdec)
    m0 = jnp.broadcast_to(st_m[..., None], st_m.shape + (LANES,))
    hcat, C, n, m, conv, S = _even_mixer(
        p.reshape(B, T, PROJ_MAIN_A), gates.reshape(B, T, LANES), jnp.cos(ang), jnp.sin(ang),
        consts, st_C, st_n, m0, st_conv, st_S, TB, L)
    x2 = _out_proj(hcat.reshape(M, D_M + D_R), W["w_out_a"], x2, W["norm_post"][0:1])

    tm = min(M, 512)
    n_rep = max(1, tm // T)
    ra, rb, rc = _partial_rope_tables(jnp.tile(pos, n_rep))
    k_transposed = T % tm == 0
    kt_tables = None
    if k_transposed:
        ang_c = _rope_angles(pos, ROPE_DIM, ROPE_THETA)
        kt_tables = (W["w_k_t"], jnp.cos(ang_c).T, jnp.sin(ang_c).T, T)
    q, kf, kb, vf, vb, z = _in_proj_c(x2, W["norm_pre"][1:2], W["w_in_c"], ra, rb, rc, tm, kt_tables)
    if k_transposed:
        k_out = jnp.transpose(kf.reshape(B, 2 * H_C, HD_C, T), (0, 3, 1, 2))[None]
    else:
        kf = kf.reshape(B, T, D_QKC)
        k_out = kf.reshape(1, B, T, 2 * H_C, HD_C)
    a = attend(q.reshape(B, T, D_QKC), kf, kb.reshape(B, T, D_QKC),
               vf.reshape(B, T, D_C), vb.reshape(B, T, D_C), z.reshape(B, T, D_C))
    y = _out_proj(a.reshape(M, D_C), W["w_out_c"], x2, W["norm_post"][1:2])
    return (y.reshape(B, T, D_MODEL), C[None], n[None], m[None, :, :, 0], conv[None], S[None],
            k_out, vf.reshape(1, B, T, H_C, 2 * HD_C))


def kernel(x_prompt, x_sample, state_mlstm_C, state_mlstm_n, state_mlstm_m, state_mlstm_conv, state_ret_S, cache_k, cache_v, page_table, norm_pre, norm_post, w_in_a, b_gate_i, b_gate_f, conv_w, conv_b, gn_mlstm, gn_ret, w_out_a, w_in_c, lambda_qk, gn_diff, w_out_c):
    g0 = 2 * D_QKM + 2 * D_M
    wa = w_in_a[0]
    W = {
        "norm_pre": norm_pre, "norm_post": norm_post,
        "w_main_a": jnp.concatenate([wa[:, :g0], wa[:, g0 + 2 * H_M:]], axis=1).astype(BF16),
        "w_gate_a": jnp.pad(wa[:, g0:g0 + 2 * H_M], ((0, 0), (0, LANES - 2 * H_M))).astype(BF16),
        "gbias": jnp.pad(jnp.concatenate([b_gate_i[0], b_gate_f[0]])[None], ((0, 0), (0, LANES - 2 * H_M))),
        "conv_w": conv_w[0], "conv_b": conv_b[0][None],
        "gn_m": gn_mlstm[0].reshape(1, D_M), "gn_r": gn_ret[0].reshape(1, D_R),
        "w_out_a": w_out_a[0].astype(BF16),
        "w_in_c": w_in_c[0].astype(BF16),
        "w_k_t": w_in_c[0][:, D_QKC:2 * D_QKC].T.astype(BF16),
        "w_out_c": w_out_c[0].astype(BF16),
    }
    lam_init = 0.8 - 0.6 * math.exp(-0.3 * 1)
    lq = lambda_qk[0]
    gn_d = gn_diff[0]

    Bp, Tp, _ = x_prompt.shape

    def attend_prompt(q, kf, kb, vf, vb, z):
        return _diff_attn_prompt(lq, q, kb, vb, z, gn_d[:, None, :], lam_init, tq=256, G=2)

    outs_p = _forward(
        x_prompt, 0,
        jnp.zeros((Bp, H_M, DK_M, DV_M), F32), jnp.zeros((Bp, H_M, DK_M), F32),
        jnp.zeros((Bp, H_M), F32), jnp.zeros((Bp, CONV_W - 1, 2 * D_QKM), F32),
        jnp.zeros((Bp, H_R, DK_R, DV_R), F32), attend_prompt, W)

    Bs, Ts, _ = x_sample.shape
    n_pool, page_size = cache_k.shape[1], cache_k.shape[2]
    past = page_table.shape[1] * page_size
    ck = jnp.transpose(cache_k[0], (0, 2, 3, 1)).reshape(n_pool, D_QKC, page_size)
    cv = cache_v[0].reshape(n_pool, page_size * H_C, 2 * HD_C)

    def attend_sample(q, kf, kb, vf, vb, z):
        head_of_lane = jnp.arange(D_QKC) // HD_C
        sel = (head_of_lane[None, :] == jnp.arange(2 * H_C)[:, None])
        qrows = jnp.where(sel[None, :, None, :], q[:, None, :, :], jnp.zeros((), q.dtype))
        qrows = qrows.reshape(Bs, 2 * H_C * Ts, D_QKC)
        n_new = NEW_TOKEN_ROWS
        kn = jnp.pad(jnp.transpose(kf, (0, 2, 1)), ((0, 0), (0, 0), (0, n_new - Ts)))
        vn = jnp.pad(vf, ((0, 0), (0, n_new - Ts), (0, 0))).reshape(Bs, n_new * H_C, 2 * HD_C)
        tok = jnp.arange(2 * H_C * Ts) % Ts
        col = jnp.arange(n_new)
        bias = jnp.where(col[None, :] <= tok[:, None], 0.0, -jnp.inf).astype(F32)
        return _paged_attn(page_table, lq, qrows, ck, cv, kn, vn, bias, z, gn_d, lam_init, P=8)

    outs_s = _forward(x_sample, past, state_mlstm_C[0], state_mlstm_n[0], state_mlstm_m[0],
                      state_mlstm_conv[0], state_ret_S[0], attend_sample, W)

    return (outs_p[0], outs_s[0]) + tuple(outs_p[1:]) + tuple(outs_s[1:])
```

```python
import functools
import math

import jax
import jax.numpy as jnp
from jax import lax
from jax.experimental import pallas as pl
from jax.experimental.pallas import tpu as pltpu

F32 = jnp.float32
BF16 = jnp.bfloat16

D_MODEL = 1024
H_M, DK_M, DV_M = 4, 256, 256
D_QKM = H_M * DK_M
D_M = H_M * DV_M
CONV_W = 4
H_R, DK_R, DV_R = 4, 256, 256
D_QKR = H_R * DK_R
D_R = H_R * DV_R
RET_THETA = 10000.0
H_C, HD_C = 8, 64
D_QKC = 2 * H_C * HD_C
D_C = H_C * 2 * HD_C
ROPE_THETA = 500000.0
ROPE_DIM = HD_C // 4
CHUNK = 256
EPS = 1e-6
PROJ_MAIN_A = 2 * D_QKM + 2 * D_M + 2 * D_QKR + 2 * D_R
LANES = 128
LOG2E = math.log2(math.e)
NEW_TOKEN_ROWS = 16
VMEM_LIMIT = 52 * 1024 * 1024
SUBLANES = 8

IN_PROJ_EVEN_TM, IN_PROJ_EVEN_TN = 1024, 2048
IN_PROJ_ODD_TM = 512
OUT_PROJ_TM = 1024
CONV_LANE_GROUP = 256
ATTN_TQ = 256
ATTN_HEADS_PER_STEP = 2
PAGES_PER_STEP = 16

OFF_QKM, OFF_VM, OFF_ZM = 0, 2 * D_QKM, 2 * D_QKM + D_M
OFF_QR = 2 * D_QKM + 2 * D_M
OFF_KR, OFF_VR, OFF_ZR = OFF_QR + D_QKR, OFF_QR + 2 * D_QKR, OFF_QR + 2 * D_QKR + D_R


def _params(sem):
    return pltpu.CompilerParams(dimension_semantics=sem, vmem_limit_bytes=VMEM_LIMIT)


def _rms(x, g):
    return x * lax.rsqrt(jnp.mean(x * x, axis=-1, keepdims=True) + EPS) * g


def _silu(x):
    return x * jax.nn.sigmoid(x)


def _dot(a, b):
    return jnp.dot(a, b, preferred_element_type=F32)


def _dot_nt(a, b):
    return lax.dot_general(a, b, (((1,), (1,)), ((), ())), preferred_element_type=F32)


def _dot_tn(a, b):
    return lax.dot_general(a, b, (((0,), (0,)), ((), ())), preferred_element_type=F32)


def _in_proj_a_kernel(x_ref, g_ref, w_ref, wg_ref, o_ref, og_ref, xn_ref):
    @pl.when(pl.program_id(1) == 0)
    def _():
        xn = _rms(x_ref[...], g_ref[...]).astype(BF16)
        xn_ref[...] = xn
        og_ref[...] = _dot(xn, wg_ref[...])

    o_ref[...] = _dot(xn_ref[...], w_ref[...]).astype(o_ref.dtype)


def _in_proj_a(x, g, w_main, w_gate, out_dtype):
    M, K = x.shape
    N = w_main.shape[1]
    tm = min(M, IN_PROJ_EVEN_TM)
    tn = IN_PROJ_EVEN_TN
    return pl.pallas_call(
        _in_proj_a_kernel,
        grid=(M // tm, N // tn),
        in_specs=[
            pl.BlockSpec((tm, K), lambda i, j: (i, 0)),
            pl.BlockSpec((1, K), lambda i, j: (0, 0)),
            pl.BlockSpec((K, tn), lambda i, j: (0, j)),
            pl.BlockSpec((K, LANES), lambda i, j: (0, 0)),
        ],
        out_specs=[
            pl.BlockSpec((tm, tn), lambda i, j: (i, j)),
            pl.BlockSpec((tm, LANES), lambda i, j: (i, 0)),
        ],
        out_shape=[
            jax.ShapeDtypeStruct((M, N), out_dtype),
            jax.ShapeDtypeStruct((M, LANES), F32),
        ],
        scratch_shapes=[pltpu.VMEM((tm, K), BF16)],
        compiler_params=_params(("parallel", "arbitrary")),
        name="in_proj_even",
    )(x, g, w_main, w_gate)


def _even_mixer_kernel(p_ref, g_ref, cos_ref, sin_ref, gb_ref, cw_ref, cb_ref, gnm_ref, gnr_ref,
                       dec_ref, rtab_ref, cdec_ref, C0_ref, n0_ref, m0_ref, cv0_ref, S0_ref,
                       h_ref, C_ref, n_ref, m_ref, cv_ref, S_ref, xbuf, qkc, qkr, *, TB, L):
    t = pl.program_id(1)
    HIST = SUBLANES

    @pl.when(t == 0)
    def _():
        C_ref[...] = C0_ref[...]
        n_ref[...] = n0_ref[...]
        m_ref[...] = m0_ref[...]
        S_ref[...] = S0_ref[...]
        xbuf[HIST - (CONV_W - 1):HIST, :] = cv0_ref[0]

    mxu_shift = p_ref.dtype == BF16 and TB % 16 == 0
    if mxu_shift:
        ti = lax.broadcasted_iota(jnp.int32, (TB, TB), 0)
        si = lax.broadcasted_iota(jnp.int32, (TB, TB), 1)
        shifts = [jnp.where(ti - si == CONV_W - 1 - d, 1.0, 0.0).astype(BF16) for d in range(CONV_W)]
    else:
        @pl.when(t > 0)
        def _():
            xbuf[0:HIST, :] = xbuf[TB:TB + HIST, :]

    for cg in range(2 * D_QKM // CONV_LANE_GROUP):
        cs = slice(cg * CONV_LANE_GROUP, (cg + 1) * CONV_LANE_GROUP)
        scale = DK_M ** -0.5 if cg < D_QKM // CONV_LANE_GROUP else 1.0
        if mxu_shift:
            xb = p_ref[0, :, cs]
            taps = [_dot(sh, xb) for sh in shifts]
            acc = cb_ref[:, cs] + taps[0] * cw_ref[0:1, cs]
            for d in range(1, CONV_W):
                acc = acc + taps[d] * cw_ref[d:d + 1, cs]
            qkc[0:TB, cs] = _silu(acc) * scale
            xbuf[HIST:2 * HIST, cs] = taps[CONV_W - 1][0:HIST]
            head = cb_ref[:, cs] + xbuf[HIST - 3:2 * HIST - 3, cs] * cw_ref[0:1, cs]
            for d in range(1, CONV_W):
                head = head + xbuf[HIST - 3 + d:2 * HIST - 3 + d, cs] * cw_ref[d:d + 1, cs]
            qkc[0:HIST, cs] = _silu(head) * scale
            xbuf[0:HIST, cs] = taps[CONV_W - 1][TB - HIST:TB]
        else:
            x = p_ref[0, :, cs].astype(F32)
            xbuf[HIST:HIST + TB, cs] = x
            acc = cb_ref[:, cs] + xbuf[HIST - 3:HIST - 3 + TB, cs] * cw_ref[0:1, cs]
            acc = acc + xbuf[HIST - 2:HIST - 2 + TB, cs] * cw_ref[1:2, cs]
            acc = acc + xbuf[HIST - 1:HIST - 1 + TB, cs] * cw_ref[2:3, cs]
            acc = acc + x * cw_ref[3:4, cs]
            qkc[0:TB, cs] = _silu(acc) * scale
    if mxu_shift:
        cv_ref[0] = xbuf[HIST - (CONV_W - 1):HIST, :]
    else:
        cv_ref[0] = xbuf[HIST + TB - (CONV_W - 1):HIST + TB, :]

    cos = cos_ref[...]
    sin = sin_ref[...]
    half = DK_R // 2
    for which, off, scale in ((0, OFF_QR, 1.0), (1, OFF_KR, DK_R ** -0.5)):
        for h in range(H_R):
            x1 = p_ref[0, :, off + h * DK_R:off + h * DK_R + half].astype(F32)
            x2 = p_ref[0, :, off + h * DK_R + half:off + (h + 1) * DK_R].astype(F32)
            base = which * D_QKR + h * DK_R
            qkr[0:TB, base:base + half] = (x1 * cos - x2 * sin) * scale
            qkr[0:TB, base + half:base + DK_R] = (x1 * sin + x2 * cos) * scale

    def chunk(rows):
        gates = g_ref[0, rows, :] + gb_ref[...]
        ti = lax.broadcasted_iota(jnp.int32, (L, L), 0)
        si = lax.broadcasted_iota(jnp.int32, (L, L), 1)
        eye = ti == si
        low = si <= ti
        upp = ti <= si
        for h in range(H_M):
            ig = gates[:, h:h + 1]
            lf = jax.nn.log_sigmoid(gates[:, H_M + h:H_M + h + 1])
            lf_r = jnp.sum(jnp.where(eye, lf, 0.0), axis=0, keepdims=True)
            i_r = jnp.sum(jnp.where(eye, ig, 0.0), axis=0, keepdims=True)
            b_c = jnp.sum(jnp.where(low, lf_r, 0.0), axis=1, keepdims=True)
            b_r = jnp.sum(jnp.where(upp, lf, 0.0), axis=0, keepdims=True)
            m_prev = m_ref[0, h:h + 1, 0:1]
            inter = b_c + m_prev
            Dm = jnp.where(low, b_c - b_r + i_r, -jnp.inf)
            m_t = jnp.maximum(inter, jnp.max(Dm, axis=1, keepdims=True))
            w = jnp.exp(Dm - m_t)
            s_int = jnp.exp(inter - m_t)
            q = qkc[rows, h * DK_M:(h + 1) * DK_M]
            k = qkc[rows, D_QKM + h * DK_M:D_QKM + (h + 1) * DK_M]
            vb = p_ref[0, rows, OFF_VM + h * DV_M:OFF_VM + (h + 1) * DV_M].astype(BF16)
            qb = q.astype(BF16)
            qk = _dot_nt(qb, k.astype(BF16)) * w
            C = C_ref[0, h]
            nvec = n_ref[0, h:h + 1, :]
            num = _dot(qk.astype(BF16), vb) + s_int * _dot(qb, C.astype(BF16))
            den = (jnp.sum(qk, axis=1, keepdims=True)
                   + s_int * jnp.sum(q * nvec, axis=1, keepdims=True))
            hh = num / jnp.maximum(jnp.abs(den), jnp.exp(-m_t))
            y = _rms(hh, gnm_ref[:, h * DV_M:(h + 1) * DV_M])
            z = p_ref[0, rows, OFF_ZM + h * DV_M:OFF_ZM + (h + 1) * DV_M].astype(F32)
            h_ref[0, rows, h * DV_M:(h + 1) * DV_M] = (y * _silu(z)).astype(h_ref.dtype)
            bL = b_c[L - 1:L, :]
            dec = bL - b_c + ig
            m_new = jnp.maximum(bL + m_prev, jnp.max(dec, axis=0, keepdims=True))
            ws = jnp.exp(dec - m_new)
            sc = jnp.exp(bL + m_prev - m_new)
            kw = k * ws
            C_ref[0, h] = sc * C + _dot_tn(kw.astype(BF16), vb)
            n_ref[0, h:h + 1, :] = sc * nvec + jnp.sum(kw, axis=0, keepdims=True)
            m_ref[0, h:h + 1, :] = jnp.broadcast_to(m_new, (1, LANES))
        for h in range(H_R):
            q = qkr[rows, h * DK_R:(h + 1) * DK_R]
            k = qkr[rows, D_QKR + h * DK_R:D_QKR + (h + 1) * DK_R]
            vb = p_ref[0, rows, OFF_VR + h * DV_R:OFF_VR + (h + 1) * DV_R].astype(BF16)
            qk = _dot_nt(q.astype(BF16), k.astype(BF16)) * dec_ref[h]
            S = S_ref[0, h]
            q_in = rtab_ref[:, h:h + 1]
            k_out = rtab_ref[:, H_R + h:H_R + h + 1]
            o = _dot(qk.astype(BF16), vb) + _dot((q * q_in).astype(BF16), S.astype(BF16))
            S_ref[0, h] = cdec_ref[:, h:h + 1] * S + _dot_tn((k * k_out).astype(BF16), vb)
            y = _rms(o, gnr_ref[:, h * DV_R:(h + 1) * DV_R])
            z = p_ref[0, rows, OFF_ZR + h * DV_R:OFF_ZR + (h + 1) * DV_R].astype(F32)
            h_ref[0, rows, D_M + h * DV_R:D_M + (h + 1) * DV_R] = (y * _silu(z)).astype(h_ref.dtype)

    n_chunks = TB // L
    if n_chunks == 1:
        chunk(slice(0, L))
    else:
        def body(c, carry):
            chunk(pl.ds(pl.multiple_of(c * L, L), L))
            return carry
        lax.fori_loop(0, n_chunks, body, 0)


def _even_mixer(p, gates, cos_r, sin_r, consts, C0, n0, m0, cv0, S0, TB, L):
    B, T, _ = p.shape
    gbias, cw, cb, gnm, gnr, decay, rtab, cdec = consts
    nt = T // TB
    tb8 = -(-TB // 8) * 8
    full = lambda shape: pl.BlockSpec(shape, lambda b, t: (0,) * len(shape))
    per_b = lambda shape: pl.BlockSpec((1,) + shape, lambda b, t: (b,) + (0,) * len(shape))
    kern = functools.partial(_even_mixer_kernel, TB=TB, L=L)
    return pl.pallas_call(
        kern,
        grid=(B, nt),
        in_specs=[
            pl.BlockSpec((1, TB, PROJ_MAIN_A), lambda b, t: (b, t, 0)),
            pl.BlockSpec((1, TB, LANES), lambda b, t: (b, t, 0)),
            pl.BlockSpec((TB, LANES), lambda b, t: (t, 0)),
            pl.BlockSpec((TB, LANES), lambda b, t: (t, 0)),
            full((1, LANES)), full((CONV_W, 2 * D_QKM)), full((1, 2 * D_QKM)),
            full((1, D_M)), full((1, D_R)), full((H_R, L, L)), full((L, LANES)), full((1, LANES)),
            per_b((H_M, DK_M, DV_M)), per_b((H_M, DK_M)), per_b((H_M, LANES)),
            per_b((CONV_W - 1, 2 * D_QKM)), per_b((H_R, DK_R, DV_R)),
        ],
        out_specs=[
            pl.BlockSpec((1, TB, D_M + D_R), lambda b, t: (b, t, 0)),
            per_b((H_M, DK_M, DV_M)), per_b((H_M, DK_M)), per_b((H_M, LANES)),
            per_b((CONV_W - 1, 2 * D_QKM)), per_b((H_R, DK_R, DV_R)),
        ],
        out_shape=[
            jax.ShapeDtypeStruct((B, T, D_M + D_R), BF16),
            jax.ShapeDtypeStruct((B, H_M, DK_M, DV_M), F32),
            jax.ShapeDtypeStruct((B, H_M, DK_M), F32),
            jax.ShapeDtypeStruct((B, H_M, LANES), F32),
            jax.ShapeDtypeStruct((B, CONV_W - 1, 2 * D_QKM), F32),
            jax.ShapeDtypeStruct((B, H_R, DK_R, DV_R), F32),
        ],
        scratch_shapes=[
            pltpu.VMEM((tb8 + 8, 2 * D_QKM), F32),
            pltpu.VMEM((tb8, 2 * D_QKM), F32),
            pltpu.VMEM((tb8, 2 * D_QKR), F32),
        ],
        compiler_params=_params(("parallel", "arbitrary")),
        name="even_mixer",
    )(p, gates, cos_r, sin_r, gbias, cw, cb, gnm, gnr, decay, rtab, cdec, C0, n0, m0, cv0, S0)


def _out_proj_kernel(h_ref, w_ref, x_ref, g_ref, y_ref):
    o = _dot(h_ref[...], w_ref[...])
    y_ref[...] = x_ref[...] + _rms(o, g_ref[...])


def _out_proj(h, w, x, g):
    M, K = h.shape
    N = w.shape[1]
    tm = min(M, OUT_PROJ_TM)
    return pl.pallas_call(
        _out_proj_kernel,
        grid=(M // tm,),
        in_specs=[
            pl.BlockSpec((tm, K), lambda i: (i, 0)),
            pl.BlockSpec((K, N), lambda i: (0, 0)),
            pl.BlockSpec((tm, N), lambda i: (i, 0)),
            pl.BlockSpec((1, N), lambda i: (0, 0)),
        ],
        out_specs=pl.BlockSpec((tm, N), lambda i: (i, 0)),
        out_shape=jax.ShapeDtypeStruct((M, N), F32),
        compiler_params=_params(("parallel",)),
        name="out_proj",
    )(h, w, x, g)


def _in_proj_c_kernel(x_ref, g_ref, w_ref, ra_ref, rb_ref, rc_ref, *rest, k_transposed):
    if k_transposed:
        wkt_ref, cos_t_ref, sin_t_ref, q_ref, kf_ref, kb_ref, vf_ref, vb_ref, z_ref = rest
    else:
        q_ref, kf_ref, kb_ref, vf_ref, vb_ref, z_ref = rest
    xn = _rms(x_ref[...], g_ref[...]).astype(BF16)
    ra = ra_ref[...]
    rb = rb_ref[...]
    rc = rc_ref[...]
    half = ROPE_DIM // 2

    def rope(y):
        outs = []
        for gi in range(y.shape[1] // LANES):
            yg = y[:, gi * LANES:(gi + 1) * LANES]
            outs.append(yg * ra + pltpu.roll(yg, half, 1) * rb + pltpu.roll(yg, LANES - half, 1) * rc)
        return jnp.concatenate(outs, axis=1)

    q = rope(_dot(xn, w_ref[:, 0:D_QKC]))
    q_ref[...] = (q * (HD_C ** -0.5 * LOG2E)).astype(BF16)
    k = rope(_dot(xn, w_ref[:, D_QKC:2 * D_QKC]))
    kb_ref[...] = k.astype(BF16)
    if k_transposed:
        kt = _dot_nt(wkt_ref[...], xn)
        cos_t = cos_t_ref[...]
        sin_t = sin_t_ref[...]
        pieces = []
        for h in range(2 * H_C):
            x1 = kt[h * HD_C:h * HD_C + half]
            x2 = kt[h * HD_C + half:h * HD_C + ROPE_DIM]
            pieces += [x1 * cos_t - x2 * sin_t, x1 * sin_t + x2 * cos_t, kt[h * HD_C + ROPE_DIM:(h + 1) * HD_C]]
        kf_ref[0] = jnp.concatenate(pieces, axis=0)
    else:
        kf_ref[...] = k
    v = _dot(xn, w_ref[:, 2 * D_QKC:2 * D_QKC + D_C])
    vf_ref[...] = v
    vb_ref[...] = v.astype(BF16)
    z_ref[...] = _dot(xn, w_ref[:, 2 * D_QKC + D_C:]).astype(BF16)


def _in_proj_c(x, g, w, ra, rb, rc, tm, kt_tables=None):
    M, K = x.shape
    n_rt = ra.shape[0] // tm
    row = lambda n: pl.BlockSpec((tm, n), lambda i: (i, 0))
    tab = pl.BlockSpec((tm, LANES), lambda i: (i % n_rt, 0))
    in_specs = [row(K), pl.BlockSpec((1, K), lambda i: (0, 0)), pl.BlockSpec(w.shape, lambda i: (0, 0)),
                tab, tab, tab]
    args = [x, g, w, ra, rb, rc]
    kf_spec, kf_shape = row(D_QKC), jax.ShapeDtypeStruct((M, D_QKC), F32)
    if kt_tables is not None:
        wkt, cos_t, sin_t, T = kt_tables
        tab_t = pl.BlockSpec((cos_t.shape[0], tm), lambda i: (0, i % n_rt))
        in_specs += [pl.BlockSpec(wkt.shape, lambda i: (0, 0)), tab_t, tab_t]
        args += [wkt, cos_t, sin_t]
        kf_spec = pl.BlockSpec((1, D_QKC, tm), lambda i: (i // n_rt, 0, i % n_rt))
        kf_shape = jax.ShapeDtypeStruct((M // T, D_QKC, T), F32)
    return pl.pallas_call(
        functools.partial(_in_proj_c_kernel, k_transposed=kt_tables is not None),
        grid=(M // tm,),
        in_specs=in_specs,
        out_specs=[row(D_QKC), kf_spec, row(D_QKC), row(D_C), row(D_C), row(D_C)],
        out_shape=[
            jax.ShapeDtypeStruct((M, D_QKC), BF16),
            kf_shape,
            jax.ShapeDtypeStruct((M, D_QKC), BF16),
            jax.ShapeDtypeStruct((M, D_C), F32),
            jax.ShapeDtypeStruct((M, D_C), BF16),
            jax.ShapeDtypeStruct((M, D_C), BF16),
        ],
        compiler_params=_params(("parallel",)),
        name="in_proj_odd",
    )(*args)


def _lambda_value(lq_ref, lam_init):
    lf = lq_ref[...]
    l1 = jnp.sum(lf[0:1] * lf[1:2], axis=1, keepdims=True)
    l2 = jnp.sum(lf[2:3] * lf[3:4], axis=1, keepdims=True)
    return jnp.exp(l1) - jnp.exp(l2) + lam_init


def _diff_attn_kernel(it_ref, jt_ref, lq_ref, q_ref, k_ref, v_ref, z_ref, gn_ref, bias_ref, o_ref,
                      vt_sc, q2_sc, s_sc, m_sc, l_sc, acc_sc, *, tq, n_steps, lam_init, G):
    nb = v_ref.shape[1] // tq
    lane = lax.broadcasted_iota(jnp.int32, (tq, LANES), 1)

    def blk(idx):
        return pl.ds(pl.multiple_of(idx * tq, tq), tq)

    def head(g):
        return slice(g * LANES, (g + 1) * LANES)

    for g in range(G):
        for jb in range(nb):
            rows = slice(jb * tq, (jb + 1) * tq)
            vt_sc[g, jb] = v_ref[0, rows, head(g)].astype(F32).T.astype(BF16)
            q = q_ref[0, rows, head(g)]
            zero = jnp.zeros_like(q)
            q2_sc[g, jb, 0:tq, :] = jnp.where(lane < HD_C, q, zero)
            q2_sc[g, jb, tq:2 * tq, :] = jnp.where(lane >= HD_C, q, zero)

    def scores(t, slot, first=False):
        for g in range(G):
            if first:
                s_sc[slot, g] = _dot_nt(k_ref[0, 0:tq, head(g)], q2_sc[g, 0])
            else:
                s_sc[slot, g] = _dot_nt(k_ref[0, blk(jt_ref[t]), head(g)], q2_sc[g, it_ref[t]])

    scores(0, 0, first=True)

    def step(t, slot):
        i = it_ref[t]
        j = jt_ref[t]

        @pl.when(j == 0)
        def _():
            m_sc[...] = jnp.full(m_sc.shape, -jnp.inf, F32)
            l_sc[...] = jnp.zeros(l_sc.shape, F32)
            acc_sc[...] = jnp.zeros(acc_sc.shape, F32)

        scores(t + 1, 1 - slot)
        bias = bias_ref[(j == i).astype(jnp.int32)]
        for g in range(G):
            s = s_sc[slot, g] + bias
            m_prev = m_sc[g]
            m_new = jnp.maximum(m_prev, jnp.max(s, axis=0, keepdims=True))
            p = jnp.exp2(s - m_new)
            alpha = jnp.exp2(m_prev - m_new)
            l_sc[g] = alpha * l_sc[g] + jnp.sum(p, axis=0, keepdims=True)
            acc_sc[g] = alpha * acc_sc[g] + _dot(vt_sc[g, j], p.astype(BF16))
            m_sc[g] = m_new

        @pl.when(j == i)
        def _():
            lam = _lambda_value(lq_ref, lam_init)
            for g in range(G):
                out_t = acc_sc[g] / l_sc[g]
                a_t = out_t[:, 0:tq] - lam * out_t[:, tq:2 * tq]
                a_t = a_t * lax.rsqrt(jnp.mean(a_t * a_t, axis=0, keepdims=True) + EPS)
                y = a_t.T * gn_ref[g] * (1.0 - lam_init)
                z = z_ref[0, blk(i), head(g)].astype(F32)
                o_ref[0, blk(i), head(g)] = (y * _silu(z)).astype(o_ref.dtype)

    def body(u, carry):
        step(2 * u, 0)
        step(2 * u + 1, 1)
        return carry

    assert n_steps % 2 == 0
    lax.fori_loop(0, n_steps // 2, body, 0)


def _diff_attn_prompt(lq, q, k, v, z, gn, lam_init, tq, G):
    B, T, _ = q.shape
    nq = T // tq
    pairs = [(i, j) for i in range(nq) for j in range(i + 1)]
    n_steps = len(pairs)
    pairs.append((0, 0))
    i_tab = jnp.asarray([p[0] for p in pairs], jnp.int32)
    j_tab = jnp.asarray([p[1] for p in pairs], jnp.int32)
    qpos = jnp.arange(2 * tq) % tq
    causal = jnp.where(jnp.arange(tq)[:, None] <= qpos[None, :], 0.0, -jnp.inf).astype(F32)
    bias = jnp.stack([jnp.zeros_like(causal), causal])
    kern = functools.partial(_diff_attn_kernel, tq=tq, n_steps=n_steps, lam_init=lam_init, G=G)
    seq = pl.BlockSpec((1, T, G * LANES), lambda b, h, it, jt: (b, 0, h))
    grid_spec = pltpu.PrefetchScalarGridSpec(
        num_scalar_prefetch=2,
        grid=(B, H_C // G),
        in_specs=[
            pl.BlockSpec(lq.shape, lambda b, h, it, jt: (0, 0)),
            seq, seq, seq, seq,
            pl.BlockSpec((G, 1, LANES), lambda b, h, it, jt: (h, 0, 0)),
            pl.BlockSpec(bias.shape, lambda b, h, it, jt: (0, 0, 0)),
        ],
        out_specs=seq,
        scratch_shapes=[
            pltpu.VMEM((G, nq, LANES, tq), BF16),
            pltpu.VMEM((G, nq, 2 * tq, LANES), BF16),
            pltpu.VMEM((2, G, tq, 2 * tq), F32),
            pltpu.VMEM((G, 1, 2 * tq), F32),
            pltpu.VMEM((G, 1, 2 * tq), F32),
            pltpu.VMEM((G, LANES, 2 * tq), F32),
        ],
    )
    return pl.pallas_call(
        kern,
        grid_spec=grid_spec,
        out_shape=jax.ShapeDtypeStruct((B, T, D_C), BF16),
        compiler_params=_params(("parallel", "parallel")),
        name="diff_attn_prompt",
    )(i_tab, j_tab, lq, q, k, v, z, gn, bias)


def _paged_attn_kernel(pt_ref, lq_ref, qr_ref, *rest, P, lam_init, Tn):
    k_refs = rest[:P]
    v_refs = rest[P:2 * P]
    kn_ref, vn_ref, bias_ref, z_ref, gn_ref, o_ref, m_sc, l_sc, acc_sc = rest[2 * P:]
    j = pl.program_id(1)
    GR = 2 * Tn

    @pl.when(j == 0)
    def _():
        m_sc[...] = jnp.full(m_sc.shape, -jnp.inf, F32)
        l_sc[...] = jnp.zeros(l_sc.shape, F32)
        acc_sc[...] = jnp.zeros(acc_sc.shape, F32)

    qr = qr_ref[0]

    def pages(kt_refs, vp_refs, bias):
        tiles = [_dot(qr, kt_ref[0].astype(BF16)) for kt_ref in kt_refs]
        if bias is not None:
            tiles = [x + bias for x in tiles]
        m_prev = m_sc[...]
        m_new = jnp.maximum(m_prev, jnp.max(functools.reduce(jnp.maximum, tiles), axis=1, keepdims=True))
        alpha = jnp.exp2(m_prev - m_new)
        m_tile = m_new[:, 0:tiles[0].shape[1]]
        ps = [jnp.exp2(x - m_tile) for x in tiles]
        l_sc[...] = alpha * l_sc[...] + jnp.sum(functools.reduce(jnp.add, ps), axis=1, keepdims=True)
        m_sc[...] = m_new
        for hp in range(H_C):
            rs = slice(hp * GR, (hp + 1) * GR)
            p_h = jnp.concatenate([x[rs, :].astype(BF16) for x in ps], axis=1)
            v_h = jnp.concatenate(
                [vp_ref[0, pl.ds(hp, vp_ref.shape[1] // H_C, stride=H_C), :].astype(BF16) for vp_ref in vp_refs],
                axis=0)
            acc_sc[rs, :] = alpha[rs, :] * acc_sc[rs, :] + _dot(p_h, v_h)

    pages(k_refs, v_refs, None)

    @pl.when(j == pl.num_programs(1) - 1)
    def _():
        pages([kn_ref], [vn_ref], bias_ref[...])
        out = acc_sc[...] / l_sc[...]
        lam = _lambda_value(lq_ref, lam_init)
        for hp in range(H_C):
            a = out[hp * GR:hp * GR + Tn] - lam * out[hp * GR + Tn:(hp + 1) * GR]
            y = _rms(a, gn_ref[hp:hp + 1, :]) * (1.0 - lam_init)
            z = z_ref[0, :, hp * LANES:(hp + 1) * LANES].astype(F32)
            o_ref[0, :, hp * LANES:(hp + 1) * LANES] = (y * _silu(z)).astype(o_ref.dtype)


def _paged_attn(page_table, lq, qrows, ck, cv, kn, vn, bias, z, gn, lam_init, P):
    B, R, _ = qrows.shape
    n_pages = page_table.shape[1]
    Tn = z.shape[1]
    kern = functools.partial(_paged_attn_kernel, P=P, lam_init=lam_init, Tn=Tn)

    def page_spec(arr, pi):
        return pl.BlockSpec((1,) + arr.shape[1:], lambda b, j, pt: (pt[b, j * P + pi], 0, 0))

    per_b = lambda shape: pl.BlockSpec((1,) + shape, lambda b, j, pt: (b,) + (0,) * len(shape))
    full = lambda shape: pl.BlockSpec(shape, lambda b, j, pt: (0,) * len(shape))
    grid_spec = pltpu.PrefetchScalarGridSpec(
        num_scalar_prefetch=1,
        grid=(B, n_pages // P),
        in_specs=[full(lq.shape), per_b(qrows.shape[1:])]
        + [page_spec(ck, pi) for pi in range(P)] + [page_spec(cv, pi) for pi in range(P)]
        + [per_b(kn.shape[1:]), per_b(vn.shape[1:]), full(bias.shape), per_b(z.shape[1:]), full(gn.shape)],
        out_specs=per_b((Tn, D_C)),
        scratch_shapes=[pltpu.VMEM((R, LANES), F32)] * 3,
    )
    return pl.pallas_call(
        kern,
        grid_spec=grid_spec,
        out_shape=jax.ShapeDtypeStruct((B, Tn, D_C), BF16),
        compiler_params=_params(("parallel", "arbitrary")),
        name="paged_diff_attn",
    )(page_table, lq, qrows, *([ck] * P), *([cv] * P), kn, vn, bias, z, gn)


def _rope_angles(pos, rot_dim, theta):
    half = rot_dim // 2
    inv = jnp.power(theta, -jnp.arange(half, dtype=F32) / half)
    return pos.astype(F32)[:, None] * inv[None, :]


def _retention_tables(L):
    lg = jnp.log(1.0 - jnp.power(2.0, -5.0 - jnp.arange(H_R, dtype=F32)))
    j = jnp.arange(L, dtype=F32)
    rel = j[:, None] - j[None, :]
    decay = jnp.where((rel >= 0)[None], jnp.exp(rel[None] * lg[:, None, None]), 0.0)
    q_in = jnp.exp((j + 1.0)[:, None] * lg)
    k_out = jnp.exp((L - 1.0 - j)[:, None] * lg)
    rtab = jnp.zeros((L, LANES), F32).at[:, 0:H_R].set(q_in).at[:, H_R:2 * H_R].set(k_out)
    cdec = jnp.zeros((1, LANES), F32).at[0, 0:H_R].set(jnp.exp(L * lg))
    return decay, rtab, cdec


def _partial_rope_tables(pos):
    ang = _rope_angles(pos, ROPE_DIM, ROPE_THETA)
    half = ROPE_DIM // 2
    cos, sin = jnp.cos(ang), jnp.sin(ang)
    T = pos.shape[0]
    ra = jnp.ones((T, HD_C), F32).at[:, 0:half].set(cos).at[:, half:ROPE_DIM].set(cos)
    rb = jnp.zeros((T, HD_C), F32).at[:, half:ROPE_DIM].set(sin)
    rc = jnp.zeros((T, HD_C), F32).at[:, 0:half].set(-sin)
    rep = LANES // HD_C
    return jnp.tile(ra, (1, rep)), jnp.tile(rb, (1, rep)), jnp.tile(rc, (1, rep))


def _forward(x, pos0, st_C, st_n, st_m, st_conv, st_S, attend, W):
    B, T, _ = x.shape
    M = B * T
    act_dtype = BF16 if T >= CHUNK else F32
    pos = pos0 + jnp.arange(T)

    x2 = x.reshape(M, D_MODEL)
    p, gates = _in_proj_a(x2, W["norm_pre"][0:1], W["w_main_a"], W["w_gate_a"], act_dtype)
    L = CHUNK if T % CHUNK == 0 else T
    TB = L
    ang = _rope_angles(pos, DK_R, RET_THETA)
    decay, rtab, cdec = _retention_tables(L)
    consts = (W["gbias"], W["conv_w"], W["conv_b"], W["gn_m"], W["gn_r"], decay, rtab, cdec)
    m0 = jnp.broadcast_to(st_m[..., None], st_m.shape + (LANES,))
    hcat, C, n, m, conv, S = _even_mixer(
        p.reshape(B, T, PROJ_MAIN_A), gates.reshape(B, T, LANES), jnp.cos(ang), jnp.sin(ang),
        consts, st_C, st_n, m0, st_conv, st_S, TB, L)
    x2 = _out_proj(hcat.reshape(M, D_M + D_R), W["w_out_a"], x2, W["norm_post"][0:1])

    tm = min(M, IN_PROJ_ODD_TM)
    n_rep = max(1, tm // T)
    ra, rb, rc = _partial_rope_tables(jnp.tile(pos, n_rep))
    k_transposed = T % tm == 0
    kt_tables = None
    if k_transposed:
        ang_c = _rope_angles(pos, ROPE_DIM, ROPE_THETA)
        kt_tables = (W["w_k_t"], jnp.cos(ang_c).T, jnp.sin(ang_c).T, T)
    q, kf, kb, vf, vb, z = _in_proj_c(x2, W["norm_pre"][1:2], W["w_in_c"], ra, rb, rc, tm, kt_tables)
    if k_transposed:
        k_out = jnp.transpose(kf.reshape(B, 2 * H_C, HD_C, T), (0, 3, 1, 2))[None]
    else:
        kf = kf.reshape(B, T, D_QKC)
        k_out = kf.reshape(1, B, T, 2 * H_C, HD_C)
    a = attend(q.reshape(B, T, D_QKC), kf, kb.reshape(B, T, D_QKC),
               vf.reshape(B, T, D_C), vb.reshape(B, T, D_C), z.reshape(B, T, D_C))
    y = _out_proj(a.reshape(M, D_C), W["w_out_c"], x2, W["norm_post"][1:2])
    return (y.reshape(B, T, D_MODEL), C[None], n[None], m[None, :, :, 0], conv[None], S[None],
            k_out, vf.reshape(1, B, T, H_C, 2 * HD_C))


def kernel(x_prompt, x_sample, state_mlstm_C, state_mlstm_n, state_mlstm_m, state_mlstm_conv, state_ret_S, cache_k, cache_v, page_table, norm_pre, norm_post, w_in_a, b_gate_i, b_gate_f, conv_w, conv_b, gn_mlstm, gn_ret, w_out_a, w_in_c, lambda_qk, gn_diff, w_out_c):
    g0 = 2 * D_QKM + 2 * D_M
    wa = w_in_a[0]
    W = {
        "norm_pre": norm_pre, "norm_post": norm_post,
        "w_main_a": jnp.concatenate([wa[:, :g0], wa[:, g0 + 2 * H_M:]], axis=1).astype(BF16),
        "w_gate_a": jnp.pad(wa[:, g0:g0 + 2 * H_M], ((0, 0), (0, LANES - 2 * H_M))).astype(BF16),
        "gbias": jnp.pad(jnp.concatenate([b_gate_i[0], b_gate_f[0]])[None], ((0, 0), (0, LANES - 2 * H_M))),
        "conv_w": conv_w[0], "conv_b": conv_b[0][None],
        "gn_m": gn_mlstm[0].reshape(1, D_M), "gn_r": gn_ret[0].reshape(1, D_R),
        "w_out_a": w_out_a[0].astype(BF16),
        "w_in_c": w_in_c[0].astype(BF16),
        "w_k_t": w_in_c[0][:, D_QKC:2 * D_QKC].T.astype(BF16),
        "w_out_c": w_out_c[0].astype(BF16),
    }
    lam_init = 0.8 - 0.6 * math.exp(-0.3 * 1)
    lq = lambda_qk[0]
    gn_d = gn_diff[0]

    Bp, Tp, _ = x_prompt.shape

    def attend_prompt(q, kf, kb, vf, vb, z):
        return _diff_attn_prompt(lq, q, kb, vb, z, gn_d[:, None, :], lam_init, tq=ATTN_TQ,
                                 G=ATTN_HEADS_PER_STEP)

    outs_p = _forward(
        x_prompt, 0,
        jnp.zeros((Bp, H_M, DK_M, DV_M), F32), jnp.zeros((Bp, H_M, DK_M), F32),
        jnp.zeros((Bp, H_M), F32), jnp.zeros((Bp, CONV_W - 1, 2 * D_QKM), F32),
        jnp.zeros((Bp, H_R, DK_R, DV_R), F32), attend_prompt, W)

    Bs, Ts, _ = x_sample.shape
    n_pool, page_size = cache_k.shape[1], cache_k.shape[2]
    past = page_table.shape[1] * page_size
    ck = jnp.transpose(cache_k[0], (0, 2, 3, 1)).reshape(n_pool, D_QKC, page_size)
    cv = cache_v[0].reshape(n_pool, page_size * H_C, 2 * HD_C)

    def attend_sample(q, kf, kb, vf, vb, z):
        head_of_lane = jnp.arange(D_QKC) // HD_C
        sel = (head_of_lane[None, :] == jnp.arange(2 * H_C)[:, None])
        qrows = jnp.where(sel[None, :, None, :], q[:, None, :, :], jnp.zeros((), q.dtype))
        qrows = qrows.reshape(Bs, 2 * H_C * Ts, D_QKC)
        n_new = NEW_TOKEN_ROWS
        kn = jnp.pad(jnp.transpose(kf, (0, 2, 1)), ((0, 0), (0, 0), (0, n_new - Ts)))
        vn = jnp.pad(vf, ((0, 0), (0, n_new - Ts), (0, 0))).reshape(Bs, n_new * H_C, 2 * HD_C)
        tok = jnp.arange(2 * H_C * Ts) % Ts
        col = jnp.arange(n_new)
        bias = jnp.where(col[None, :] <= tok[:, None], 0.0, -jnp.inf).astype(F32)
        return _paged_attn(page_table, lq, qrows, ck, cv, kn, vn, bias, z, gn_d, lam_init,
                           P=PAGES_PER_STEP)

    outs_s = _forward(x_sample, past, state_mlstm_C[0], state_mlstm_n[0], state_mlstm_m[0],
                      state_mlstm_conv[0], state_ret_S[0], attend_sample, W)

    return (outs_p[0], outs_s[0]) + tuple(outs_p[1:]) + tuple(outs_s[1:])
```

```python
import functools
import math

import jax
import jax.numpy as jnp
from jax import lax
from jax.experimental import pallas as pl
from jax.experimental.pallas import tpu as pltpu

F32 = jnp.float32
BF16 = jnp.bfloat16

D_MODEL = 1024
H_M, DK_M, DV_M = 4, 256, 256
D_QKM = H_M * DK_M
D_M = H_M * DV_M
CONV_W = 4
H_R, DK_R, DV_R = 4, 256, 256
D_QKR = H_R * DK_R
D_R = H_R * DV_R
RET_THETA = 10000.0
H_C, HD_C = 8, 64
D_QKC = 2 * H_C * HD_C
D_C = H_C * 2 * HD_C
ROPE_THETA = 500000.0
ROPE_DIM = HD_C // 4
CHUNK = 256
EPS = 1e-6
PROJ_MAIN_A = 2 * D_QKM + 2 * D_M + 2 * D_QKR + 2 * D_R
LANES = 128
LOG2E = math.log2(math.e)
NEW_TOKEN_ROWS = 16
ONES_ROWS = 16
VMEM_LIMIT = 52 * 1024 * 1024
SUBLANES = 8

IN_PROJ_EVEN_TM, IN_PROJ_EVEN_TN = 1024, 2048
IN_PROJ_ODD_TM = 512
OUT_PROJ_TM = 1024
CONV_LANE_GROUP = 256
ATTN_TQ = 256
ATTN_HEADS_PER_STEP = 2
PAGES_PER_STEP = 16

OFF_QKM, OFF_VM, OFF_ZM = 0, 2 * D_QKM, 2 * D_QKM + D_M
OFF_QR = 2 * D_QKM + 2 * D_M
OFF_KR, OFF_VR, OFF_ZR = OFF_QR + D_QKR, OFF_QR + 2 * D_QKR, OFF_QR + 2 * D_QKR + D_R


def _params(sem):
    return pltpu.CompilerParams(dimension_semantics=sem, vmem_limit_bytes=VMEM_LIMIT)


def _rms(x, g):
    return x * lax.rsqrt(jnp.mean(x * x, axis=-1, keepdims=True) + EPS) * g


def _silu(x):
    return x * jax.nn.sigmoid(x)


def _dot(a, b):
    return jnp.dot(a, b, preferred_element_type=F32)


def _dot_nt(a, b):
    return lax.dot_general(a, b, (((1,), (1,)), ((), ())), preferred_element_type=F32)


def _dot_tn(a, b):
    return lax.dot_general(a, b, (((0,), (0,)), ((), ())), preferred_element_type=F32)


def _in_proj_a_kernel(x_ref, g_ref, w_ref, wg_ref, o_ref, og_ref, xn_ref):
    @pl.when(pl.program_id(1) == 0)
    def _():
        xn = _rms(x_ref[...], g_ref[...]).astype(BF16)
        xn_ref[...] = xn
        og_ref[...] = _dot(xn, wg_ref[...])

    o_ref[...] = _dot(xn_ref[...], w_ref[...]).astype(o_ref.dtype)


def _in_proj_a(x, g, w_main, w_gate, out_dtype):
    M, K = x.shape
    N = w_main.shape[1]
    tm = min(M, IN_PROJ_EVEN_TM)
    tn = IN_PROJ_EVEN_TN
    return pl.pallas_call(
        _in_proj_a_kernel,
        grid=(M // tm, N // tn),
        in_specs=[
            pl.BlockSpec((tm, K), lambda i, j: (i, 0)),
            pl.BlockSpec((1, K), lambda i, j: (0, 0)),
            pl.BlockSpec((K, tn), lambda i, j: (0, j)),
            pl.BlockSpec((K, LANES), lambda i, j: (0, 0)),
        ],
        out_specs=[
            pl.BlockSpec((tm, tn), lambda i, j: (i, j)),
            pl.BlockSpec((tm, LANES), lambda i, j: (i, 0)),
        ],
        out_shape=[
            jax.ShapeDtypeStruct((M, N), out_dtype),
            jax.ShapeDtypeStruct((M, LANES), F32),
        ],
        scratch_shapes=[pltpu.VMEM((tm, K), BF16)],
        compiler_params=_params(("parallel", "arbitrary")),
        name="in_proj_even",
    )(x, g, w_main, w_gate)


def _even_mixer_kernel(p_ref, g_ref, cos_ref, sin_ref, gb_ref, cw_ref, cb_ref, gnm_ref, gnr_ref,
                       dec_ref, rtab_ref, cdec_ref, C0_ref, n0_ref, m0_ref, cv0_ref, S0_ref,
                       h_ref, C_ref, n_ref, m_ref, cv_ref, S_ref, xbuf, qkc, qkr, *, TB, L):
    t = pl.program_id(1)
    HIST = SUBLANES

    @pl.when(t == 0)
    def _():
        C_ref[...] = C0_ref[...]
        n_ref[...] = n0_ref[...]
        m_ref[...] = m0_ref[...]
        S_ref[...] = S0_ref[...]
        xbuf[HIST - (CONV_W - 1):HIST, :] = cv0_ref[0]

    mxu_shift = p_ref.dtype == BF16 and TB % 16 == 0
    if mxu_shift:
        ti = lax.broadcasted_iota(jnp.int32, (TB, TB), 0)
        si = lax.broadcasted_iota(jnp.int32, (TB, TB), 1)
        shifts = [jnp.where(ti - si == CONV_W - 1 - d, 1.0, 0.0).astype(BF16) for d in range(CONV_W)]
    else:
        @pl.when(t > 0)
        def _():
            xbuf[0:HIST, :] = xbuf[TB:TB + HIST, :]

    for cg in range(2 * D_QKM // CONV_LANE_GROUP):
        cs = slice(cg * CONV_LANE_GROUP, (cg + 1) * CONV_LANE_GROUP)
        scale = DK_M ** -0.5 if cg < D_QKM // CONV_LANE_GROUP else 1.0
        if mxu_shift:
            xb = p_ref[0, :, cs]
            taps = [_dot(sh, xb) for sh in shifts]
            acc = cb_ref[:, cs] + taps[0] * cw_ref[0:1, cs]
            for d in range(1, CONV_W):
                acc = acc + taps[d] * cw_ref[d:d + 1, cs]
            qkc[0:TB, cs] = _silu(acc) * scale
            xbuf[HIST:2 * HIST, cs] = taps[CONV_W - 1][0:HIST]
            head = cb_ref[:, cs] + xbuf[HIST - 3:2 * HIST - 3, cs] * cw_ref[0:1, cs]
            for d in range(1, CONV_W):
                head = head + xbuf[HIST - 3 + d:2 * HIST - 3 + d, cs] * cw_ref[d:d + 1, cs]
            qkc[0:HIST, cs] = _silu(head) * scale
            xbuf[0:HIST, cs] = taps[CONV_W - 1][TB - HIST:TB]
        else:
            x = p_ref[0, :, cs].astype(F32)
            xbuf[HIST:HIST + TB, cs] = x
            acc = cb_ref[:, cs] + xbuf[HIST - 3:HIST - 3 + TB, cs] * cw_ref[0:1, cs]
            acc = acc + xbuf[HIST - 2:HIST - 2 + TB, cs] * cw_ref[1:2, cs]
            acc = acc + xbuf[HIST - 1:HIST - 1 + TB, cs] * cw_ref[2:3, cs]
            acc = acc + x * cw_ref[3:4, cs]
            qkc[0:TB, cs] = _silu(acc) * scale
    if mxu_shift:
        cv_ref[0] = xbuf[HIST - (CONV_W - 1):HIST, :]
    else:
        cv_ref[0] = xbuf[HIST + TB - (CONV_W - 1):HIST + TB, :]

    cos = cos_ref[...]
    sin = sin_ref[...]
    half = DK_R // 2
    for which, off, scale in ((0, OFF_QR, 1.0), (1, OFF_KR, DK_R ** -0.5)):
        for h in range(H_R):
            x1 = p_ref[0, :, off + h * DK_R:off + h * DK_R + half].astype(F32)
            x2 = p_ref[0, :, off + h * DK_R + half:off + (h + 1) * DK_R].astype(F32)
            base = which * D_QKR + h * DK_R
            qkr[0:TB, base:base + half] = (x1 * cos - x2 * sin) * scale
            qkr[0:TB, base + half:base + DK_R] = (x1 * sin + x2 * cos) * scale

    def chunk(rows):
        gates = g_ref[0, rows, :] + gb_ref[...]
        ti = lax.broadcasted_iota(jnp.int32, (L, L), 0)
        si = lax.broadcasted_iota(jnp.int32, (L, L), 1)
        eye = ti == si
        low = si <= ti
        upp = ti <= si
        for h in range(H_M):
            ig = gates[:, h:h + 1]
            lf = jax.nn.log_sigmoid(gates[:, H_M + h:H_M + h + 1])
            lf_r = jnp.sum(jnp.where(eye, lf, 0.0), axis=0, keepdims=True)
            i_r = jnp.sum(jnp.where(eye, ig, 0.0), axis=0, keepdims=True)
            b_c = jnp.sum(jnp.where(low, lf_r, 0.0), axis=1, keepdims=True)
            b_r = jnp.sum(jnp.where(upp, lf, 0.0), axis=0, keepdims=True)
            m_prev = m_ref[0, h:h + 1, 0:1]
            inter = b_c + m_prev
            Dm = jnp.where(low, b_c - b_r + i_r, -jnp.inf)
            m_t = jnp.maximum(inter, jnp.max(Dm, axis=1, keepdims=True))
            w = jnp.exp(Dm - m_t)
            s_int = jnp.exp(inter - m_t)
            q = qkc[rows, h * DK_M:(h + 1) * DK_M]
            k = qkc[rows, D_QKM + h * DK_M:D_QKM + (h + 1) * DK_M]
            vb = p_ref[0, rows, OFF_VM + h * DV_M:OFF_VM + (h + 1) * DV_M].astype(BF16)
            qb = q.astype(BF16)
            qk = _dot_nt(qb, k.astype(BF16)) * w
            C = C_ref[0, h]
            nvec = n_ref[0, h:h + 1, :]
            num = _dot(qk.astype(BF16), vb) + s_int * _dot(qb, C.astype(BF16))
            den = (jnp.sum(qk, axis=1, keepdims=True)
                   + s_int * jnp.sum(q * nvec, axis=1, keepdims=True))
            hh = num / jnp.maximum(jnp.abs(den), jnp.exp(-m_t))
            y = _rms(hh, gnm_ref[:, h * DV_M:(h + 1) * DV_M])
            z = p_ref[0, rows, OFF_ZM + h * DV_M:OFF_ZM + (h + 1) * DV_M].astype(F32)
            h_ref[0, rows, h * DV_M:(h + 1) * DV_M] = (y * _silu(z)).astype(h_ref.dtype)
            bL = b_c[L - 1:L, :]
            dec = bL - b_c + ig
            m_new = jnp.maximum(bL + m_prev, jnp.max(dec, axis=0, keepdims=True))
            ws = jnp.exp(dec - m_new)
            sc = jnp.exp(bL + m_prev - m_new)
            kw = k * ws
            C_ref[0, h] = sc * C + _dot_tn(kw.astype(BF16), vb)
            n_ref[0, h:h + 1, :] = sc * nvec + jnp.sum(kw, axis=0, keepdims=True)
            m_ref[0, h:h + 1, :] = jnp.broadcast_to(m_new, (1, LANES))
        for h in range(H_R):
            q = qkr[rows, h * DK_R:(h + 1) * DK_R]
            k = qkr[rows, D_QKR + h * DK_R:D_QKR + (h + 1) * DK_R]
            vb = p_ref[0, rows, OFF_VR + h * DV_R:OFF_VR + (h + 1) * DV_R].astype(BF16)
            qk = _dot_nt(q.astype(BF16), k.astype(BF16)) * dec_ref[h]
            S = S_ref[0, h]
            q_in = rtab_ref[:, h:h + 1]
            k_out = rtab_ref[:, H_R + h:H_R + h + 1]
            o = _dot(qk.astype(BF16), vb) + _dot((q * q_in).astype(BF16), S.astype(BF16))
            S_ref[0, h] = cdec_ref[:, h:h + 1] * S + _dot_tn((k * k_out).astype(BF16), vb)
            y = _rms(o, gnr_ref[:, h * DV_R:(h + 1) * DV_R])
            z = p_ref[0, rows, OFF_ZR + h * DV_R:OFF_ZR + (h + 1) * DV_R].astype(F32)
            h_ref[0, rows, D_M + h * DV_R:D_M + (h + 1) * DV_R] = (y * _silu(z)).astype(h_ref.dtype)

    n_chunks = TB // L
    if n_chunks == 1:
        chunk(slice(0, L))
    else:
        def body(c, carry):
            chunk(pl.ds(pl.multiple_of(c * L, L), L))
            return carry
        lax.fori_loop(0, n_chunks, body, 0)


def _even_mixer(p, gates, cos_r, sin_r, consts, C0, n0, m0, cv0, S0, TB, L):
    B, T, _ = p.shape
    gbias, cw, cb, gnm, gnr, decay, rtab, cdec = consts
    nt = T // TB
    tb8 = -(-TB // 8) * 8
    full = lambda shape: pl.BlockSpec(shape, lambda b, t: (0,) * len(shape))
    per_b = lambda shape: pl.BlockSpec((1,) + shape, lambda b, t: (b,) + (0,) * len(shape))
    kern = functools.partial(_even_mixer_kernel, TB=TB, L=L)
    return pl.pallas_call(
        kern,
        grid=(B, nt),
        in_specs=[
            pl.BlockSpec((1, TB, PROJ_MAIN_A), lambda b, t: (b, t, 0)),
            pl.BlockSpec((1, TB, LANES), lambda b, t: (b, t, 0)),
            pl.BlockSpec((TB, LANES), lambda b, t: (t, 0)),
            pl.BlockSpec((TB, LANES), lambda b, t: (t, 0)),
            full((1, LANES)), full((CONV_W, 2 * D_QKM)), full((1, 2 * D_QKM)),
            full((1, D_M)), full((1, D_R)), full((H_R, L, L)), full((L, LANES)), full((1, LANES)),
            per_b((H_M, DK_M, DV_M)), per_b((H_M, DK_M)), per_b((H_M, LANES)),
            per_b((CONV_W - 1, 2 * D_QKM)), per_b((H_R, DK_R, DV_R)),
        ],
        out_specs=[
            pl.BlockSpec((1, TB, D_M + D_R), lambda b, t: (b, t, 0)),
            per_b((H_M, DK_M, DV_M)), per_b((H_M, DK_M)), per_b((H_M, LANES)),
            per_b((CONV_W - 1, 2 * D_QKM)), per_b((H_R, DK_R, DV_R)),
        ],
        out_shape=[
            jax.ShapeDtypeStruct((B, T, D_M + D_R), BF16),
            jax.ShapeDtypeStruct((B, H_M, DK_M, DV_M), F32),
            jax.ShapeDtypeStruct((B, H_M, DK_M), F32),
            jax.ShapeDtypeStruct((B, H_M, LANES), F32),
            jax.ShapeDtypeStruct((B, CONV_W - 1, 2 * D_QKM), F32),
            jax.ShapeDtypeStruct((B, H_R, DK_R, DV_R), F32),
        ],
        scratch_shapes=[
            pltpu.VMEM((tb8 + 8, 2 * D_QKM), F32),
            pltpu.VMEM((tb8, 2 * D_QKM), F32),
            pltpu.VMEM((tb8, 2 * D_QKR), F32),
        ],
        compiler_params=_params(("parallel", "arbitrary")),
        name="even_mixer",
    )(p, gates, cos_r, sin_r, gbias, cw, cb, gnm, gnr, decay, rtab, cdec, C0, n0, m0, cv0, S0)


def _out_proj_kernel(h_ref, w_ref, x_ref, g_ref, y_ref):
    o = _dot(h_ref[...], w_ref[...])
    y_ref[...] = x_ref[...] + _rms(o, g_ref[...])


def _out_proj(h, w, x, g):
    M, K = h.shape
    N = w.shape[1]
    tm = min(M, OUT_PROJ_TM)
    return pl.pallas_call(
        _out_proj_kernel,
        grid=(M // tm,),
        in_specs=[
            pl.BlockSpec((tm, K), lambda i: (i, 0)),
            pl.BlockSpec((K, N), lambda i: (0, 0)),
            pl.BlockSpec((tm, N), lambda i: (i, 0)),
            pl.BlockSpec((1, N), lambda i: (0, 0)),
        ],
        out_specs=pl.BlockSpec((tm, N), lambda i: (i, 0)),
        out_shape=jax.ShapeDtypeStruct((M, N), F32),
        compiler_params=_params(("parallel",)),
        name="out_proj",
    )(h, w, x, g)


def _in_proj_c_kernel(x_ref, g_ref, w_ref, ra_ref, rb_ref, rc_ref, *rest, k_transposed):
    if k_transposed:
        wkt_ref, cos_t_ref, sin_t_ref, q_ref, kf_ref, kb_ref, vf_ref, vb_ref, z_ref = rest
    else:
        q_ref, kf_ref, kb_ref, vf_ref, vb_ref, z_ref = rest
    xn = _rms(x_ref[...], g_ref[...]).astype(BF16)
    ra = ra_ref[...]
    rb = rb_ref[...]
    rc = rc_ref[...]
    half = ROPE_DIM // 2

    def rope(y):
        outs = []
        for gi in range(y.shape[1] // LANES):
            yg = y[:, gi * LANES:(gi + 1) * LANES]
            outs.append(yg * ra + pltpu.roll(yg, half, 1) * rb + pltpu.roll(yg, LANES - half, 1) * rc)
        return jnp.concatenate(outs, axis=1)

    q = rope(_dot(xn, w_ref[:, 0:D_QKC]))
    q_ref[...] = (q * (HD_C ** -0.5 * LOG2E)).astype(BF16)
    k = rope(_dot(xn, w_ref[:, D_QKC:2 * D_QKC]))
    kb_ref[...] = k.astype(BF16)
    if k_transposed:
        kt = _dot_nt(wkt_ref[...], xn)
        cos_t = cos_t_ref[...]
        sin_t = sin_t_ref[...]
        pieces = []
        for h in range(2 * H_C):
            x1 = kt[h * HD_C:h * HD_C + half]
            x2 = kt[h * HD_C + half:h * HD_C + ROPE_DIM]
            pieces += [x1 * cos_t - x2 * sin_t, x1 * sin_t + x2 * cos_t, kt[h * HD_C + ROPE_DIM:(h + 1) * HD_C]]
        kf_ref[0] = jnp.concatenate(pieces, axis=0)
    else:
        kf_ref[...] = k
    v = _dot(xn, w_ref[:, 2 * D_QKC:2 * D_QKC + D_C])
    vf_ref[...] = v
    vb_ref[...] = v.astype(BF16)
    z_ref[...] = _dot(xn, w_ref[:, 2 * D_QKC + D_C:]).astype(BF16)


def _in_proj_c(x, g, w, ra, rb, rc, tm, kt_tables=None):
    M, K = x.shape
    n_rt = ra.shape[0] // tm
    row = lambda n: pl.BlockSpec((tm, n), lambda i: (i, 0))
    tab = pl.BlockSpec((tm, LANES), lambda i: (i % n_rt, 0))
    in_specs = [row(K), pl.BlockSpec((1, K), lambda i: (0, 0)), pl.BlockSpec(w.shape, lambda i: (0, 0)),
                tab, tab, tab]
    args = [x, g, w, ra, rb, rc]
    kf_spec, kf_shape = row(D_QKC), jax.ShapeDtypeStruct((M, D_QKC), F32)
    if kt_tables is not None:
        wkt, cos_t, sin_t, T = kt_tables
        tab_t = pl.BlockSpec((cos_t.shape[0], tm), lambda i: (0, i % n_rt))
        in_specs += [pl.BlockSpec(wkt.shape, lambda i: (0, 0)), tab_t, tab_t]
        args += [wkt, cos_t, sin_t]
        kf_spec = pl.BlockSpec((1, D_QKC, tm), lambda i: (i // n_rt, 0, i % n_rt))
        kf_shape = jax.ShapeDtypeStruct((M // T, D_QKC, T), F32)
    return pl.pallas_call(
        functools.partial(_in_proj_c_kernel, k_transposed=kt_tables is not None),
        grid=(M // tm,),
        in_specs=in_specs,
        out_specs=[row(D_QKC), kf_spec, row(D_QKC), row(D_C), row(D_C), row(D_C)],
        out_shape=[
            jax.ShapeDtypeStruct((M, D_QKC), BF16),
            kf_shape,
            jax.ShapeDtypeStruct((M, D_QKC), BF16),
            jax.ShapeDtypeStruct((M, D_C), F32),
            jax.ShapeDtypeStruct((M, D_C), BF16),
            jax.ShapeDtypeStruct((M, D_C), BF16),
        ],
        compiler_params=_params(("parallel",)),
        name="in_proj_odd",
    )(*args)


def _lambda_value(lq_ref, lam_init):
    lf = lq_ref[...]
    l1 = jnp.sum(lf[0:1] * lf[1:2], axis=1, keepdims=True)
    l2 = jnp.sum(lf[2:3] * lf[3:4], axis=1, keepdims=True)
    return jnp.exp(l1) - jnp.exp(l2) + lam_init


def _diff_attn_kernel(it_ref, jt_ref, lq_ref, q_ref, k_ref, v_ref, z_ref, gn_ref, bias_ref, o_ref,
                      vt_sc, q2_sc, s_sc, m_sc, acc_sc, *, tq, n_steps, lam_init, G):
    nb = v_ref.shape[1] // tq
    lane = lax.broadcasted_iota(jnp.int32, (tq, LANES), 1)

    def blk(idx):
        return pl.ds(pl.multiple_of(idx * tq, tq), tq)

    def head(g):
        return slice(g * LANES, (g + 1) * LANES)

    for g in range(G):
        for jb in range(nb):
            rows = slice(jb * tq, (jb + 1) * tq)
            vt_sc[g, jb, 0:LANES, :] = v_ref[0, rows, head(g)].astype(F32).T.astype(BF16)
            vt_sc[g, jb, LANES:LANES + ONES_ROWS, :] = jnp.ones((ONES_ROWS, tq), BF16)
            q = q_ref[0, rows, head(g)]
            zero = jnp.zeros_like(q)
            q2_sc[g, jb, 0:tq, :] = jnp.where(lane < HD_C, q, zero)
            q2_sc[g, jb, tq:2 * tq, :] = jnp.where(lane >= HD_C, q, zero)

    def scores(t, slot, first=False):
        for g in range(G):
            if first:
                s_sc[slot, g] = _dot_nt(k_ref[0, 0:tq, head(g)], q2_sc[g, 0])
            else:
                s_sc[slot, g] = _dot_nt(k_ref[0, blk(jt_ref[t]), head(g)], q2_sc[g, it_ref[t]])

    scores(0, 0, first=True)

    def step(t, slot):
        i = it_ref[t]
        j = jt_ref[t]

        @pl.when(j == 0)
        def _():
            m_sc[...] = jnp.full(m_sc.shape, -jnp.inf, F32)
            acc_sc[...] = jnp.zeros(acc_sc.shape, F32)

        scores(t + 1, 1 - slot)
        bias = bias_ref[(j == i).astype(jnp.int32)]
        for g in range(G):
            s = s_sc[slot, g] + bias
            m_prev = m_sc[g]
            m_new = jnp.maximum(m_prev, jnp.max(s, axis=0, keepdims=True))
            p = jnp.exp2((s - m_new).astype(BF16))
            alpha = jnp.exp2(m_prev - m_new)
            acc_sc[g] = alpha * acc_sc[g] + _dot(vt_sc[g, j], p)
            m_sc[g] = m_new

        @pl.when(j == i)
        def _():
            lam = _lambda_value(lq_ref, lam_init)
            for g in range(G):
                out_t = acc_sc[g, 0:LANES, :] / acc_sc[g, LANES:LANES + 1, :]
                a_t = out_t[:, 0:tq] - lam * out_t[:, tq:2 * tq]
                a_t = a_t * lax.rsqrt(jnp.mean(a_t * a_t, axis=0, keepdims=True) + EPS)
                y = a_t.T * gn_ref[g] * (1.0 - lam_init)
                z = z_ref[0, blk(i), head(g)].astype(F32)
                o_ref[0, blk(i), head(g)] = (y * _silu(z)).astype(o_ref.dtype)

    def body(u, carry):
        step(2 * u, 0)
        step(2 * u + 1, 1)
        return carry

    assert n_steps % 2 == 0
    lax.fori_loop(0, n_steps // 2, body, 0)


def _diff_attn_prompt(lq, q, k, v, z, gn, lam_init, tq, G):
    B, T, _ = q.shape
    nq = T // tq
    pairs = [(i, j) for i in range(nq) for j in range(i + 1)]
    n_steps = len(pairs)
    pairs.append((0, 0))
    i_tab = jnp.asarray([p[0] for p in pairs], jnp.int32)
    j_tab = jnp.asarray([p[1] for p in pairs], jnp.int32)
    qpos = jnp.arange(2 * tq) % tq
    causal = jnp.where(jnp.arange(tq)[:, None] <= qpos[None, :], 0.0, -jnp.inf).astype(F32)
    bias = jnp.stack([jnp.zeros_like(causal), causal])
    kern = functools.partial(_diff_attn_kernel, tq=tq, n_steps=n_steps, lam_init=lam_init, G=G)
    seq = pl.BlockSpec((1, T, G * LANES), lambda b, h, it, jt: (b, 0, h))
    grid_spec = pltpu.PrefetchScalarGridSpec(
        num_scalar_prefetch=2,
        grid=(B, H_C // G),
        in_specs=[
            pl.BlockSpec(lq.shape, lambda b, h, it, jt: (0, 0)),
            seq, seq, seq, seq,
            pl.BlockSpec((G, 1, LANES), lambda b, h, it, jt: (h, 0, 0)),
            pl.BlockSpec(bias.shape, lambda b, h, it, jt: (0, 0, 0)),
        ],
        out_specs=seq,
        scratch_shapes=[
            pltpu.VMEM((G, nq, LANES + ONES_ROWS, tq), BF16),
            pltpu.VMEM((G, nq, 2 * tq, LANES), BF16),
            pltpu.VMEM((2, G, tq, 2 * tq), F32),
            pltpu.VMEM((G, 1, 2 * tq), F32),
            pltpu.VMEM((G, LANES + ONES_ROWS, 2 * tq), F32),
        ],
    )
    return pl.pallas_call(
        kern,
        grid_spec=grid_spec,
        out_shape=jax.ShapeDtypeStruct((B, T, D_C), BF16),
        compiler_params=_params(("parallel", "parallel")),
        name="diff_attn_prompt",
    )(i_tab, j_tab, lq, q, k, v, z, gn, bias)


def _paged_attn_kernel(pt_ref, lq_ref, qr_ref, *rest, P, lam_init, Tn):
    k_refs = rest[:P]
    v_refs = rest[P:2 * P]
    kn_ref, vn_ref, bias_ref, z_ref, gn_ref, o_ref, m_sc, l_sc, acc_sc = rest[2 * P:]
    j = pl.program_id(1)
    GR = 2 * Tn

    @pl.when(j == 0)
    def _():
        m_sc[...] = jnp.full(m_sc.shape, -jnp.inf, F32)
        l_sc[...] = jnp.zeros(l_sc.shape, F32)
        acc_sc[...] = jnp.zeros(acc_sc.shape, F32)

    qr = qr_ref[0]

    def pages(kt_refs, vp_refs, bias):
        tiles = [_dot(qr, kt_ref[0].astype(BF16)) for kt_ref in kt_refs]
        if bias is not None:
            tiles = [x + bias for x in tiles]
        m_prev = m_sc[...]
        m_new = jnp.maximum(m_prev, jnp.max(functools.reduce(jnp.maximum, tiles), axis=1, keepdims=True))
        alpha = jnp.exp2(m_prev - m_new)
        m_tile = m_new[:, 0:tiles[0].shape[1]]
        ps = [jnp.exp2(x - m_tile) for x in tiles]
        l_sc[...] = alpha * l_sc[...] + jnp.sum(functools.reduce(jnp.add, ps), axis=1, keepdims=True)
        m_sc[...] = m_new
        for hp in range(H_C):
            rs = slice(hp * GR, (hp + 1) * GR)
            p_h = jnp.concatenate([x[rs, :].astype(BF16) for x in ps], axis=1)
            v_h = jnp.concatenate(
                [vp_ref[0, pl.ds(hp, vp_ref.shape[1] // H_C, stride=H_C), :].astype(BF16) for vp_ref in vp_refs],
                axis=0)
            acc_sc[rs, :] = alpha[rs, :] * acc_sc[rs, :] + _dot(p_h, v_h)

    pages(k_refs, v_refs, None)

    @pl.when(j == pl.num_programs(1) - 1)
    def _():
        pages([kn_ref], [vn_ref], bias_ref[...])
        out = acc_sc[...] / l_sc[...]
        lam = _lambda_value(lq_ref, lam_init)
        for hp in range(H_C):
            a = out[hp * GR:hp * GR + Tn] - lam * out[hp * GR + Tn:(hp + 1) * GR]
            y = _rms(a, gn_ref[hp:hp + 1, :]) * (1.0 - lam_init)
            z = z_ref[0, :, hp * LANES:(hp + 1) * LANES].astype(F32)
            o_ref[0, :, hp * LANES:(hp + 1) * LANES] = (y * _silu(z)).astype(o_ref.dtype)


def _paged_attn(page_table, lq, qrows, ck, cv, kn, vn, bias, z, gn, lam_init, P):
    B, R, _ = qrows.shape
    n_pages = page_table.shape[1]
    Tn = z.shape[1]
    kern = functools.partial(_paged_attn_kernel, P=P, lam_init=lam_init, Tn=Tn)

    def page_spec(arr, pi):
        return pl.BlockSpec((1,) + arr.shape[1:], lambda b, j, pt: (pt[b, j * P + pi], 0, 0))

    per_b = lambda shape: pl.BlockSpec((1,) + shape, lambda b, j, pt: (b,) + (0,) * len(shape))
    full = lambda shape: pl.BlockSpec(shape, lambda b, j, pt: (0,) * len(shape))
    grid_spec = pltpu.PrefetchScalarGridSpec(
        num_scalar_prefetch=1,
        grid=(B, n_pages // P),
        in_specs=[full(lq.shape), per_b(qrows.shape[1:])]
        + [page_spec(ck, pi) for pi in range(P)] + [page_spec(cv, pi) for pi in range(P)]
        + [per_b(kn.shape[1:]), per_b(vn.shape[1:]), full(bias.shape), per_b(z.shape[1:]), full(gn.shape)],
        out_specs=per_b((Tn, D_C)),
        scratch_shapes=[pltpu.VMEM((R, LANES), F32)] * 3,
    )
    return pl.pallas_call(
        kern,
        grid_spec=grid_spec,
        out_shape=jax.ShapeDtypeStruct((B, Tn, D_C), BF16),
        compiler_params=_params(("parallel", "arbitrary")),
        name="paged_diff_attn",
    )(page_table, lq, qrows, *([ck] * P), *([cv] * P), kn, vn, bias, z, gn)


def _rope_angles(pos, rot_dim, theta):
    half = rot_dim // 2
    inv = jnp.power(theta, -jnp.arange(half, dtype=F32) / half)
    return pos.astype(F32)[:, None] * inv[None, :]


def _retention_tables(L):
    lg = jnp.log(1.0 - jnp.power(2.0, -5.0 - jnp.arange(H_R, dtype=F32)))
    j = jnp.arange(L, dtype=F32)
    rel = j[:, None] - j[None, :]
    decay = jnp.where((rel >= 0)[None], jnp.exp(rel[None] * lg[:, None, None]), 0.0)
    q_in = jnp.exp((j + 1.0)[:, None] * lg)
    k_out = jnp.exp((L - 1.0 - j)[:, None] * lg)
    rtab = jnp.zeros((L, LANES), F32).at[:, 0:H_R].set(q_in).at[:, H_R:2 * H_R].set(k_out)
    cdec = jnp.zeros((1, LANES), F32).at[0, 0:H_R].set(jnp.exp(L * lg))
    return decay, rtab, cdec


def _partial_rope_tables(pos):
    ang = _rope_angles(pos, ROPE_DIM, ROPE_THETA)
    half = ROPE_DIM // 2
    cos, sin = jnp.cos(ang), jnp.sin(ang)
    T = pos.shape[0]
    ra = jnp.ones((T, HD_C), F32).at[:, 0:half].set(cos).at[:, half:ROPE_DIM].set(cos)
    rb = jnp.zeros((T, HD_C), F32).at[:, half:ROPE_DIM].set(sin)
    rc = jnp.zeros((T, HD_C), F32).at[:, 0:half].set(-sin)
    rep = LANES // HD_C
    return jnp.tile(ra, (1, rep)), jnp.tile(rb, (1, rep)), jnp.tile(rc, (1, rep))


def _forward(x, pos0, st_C, st_n, st_m, st_conv, st_S, attend, W):
    B, T, _ = x.shape
    M = B * T
    act_dtype = BF16 if T >= CHUNK else F32
    pos = pos0 + jnp.arange(T)

    x2 = x.reshape(M, D_MODEL)
    p, gates = _in_proj_a(x2, W["norm_pre"][0:1], W["w_main_a"], W["w_gate_a"], act_dtype)
    L = CHUNK if T % CHUNK == 0 else T
    TB = L
    ang = _rope_angles(pos, DK_R, RET_THETA)
    decay, rtab, cdec = _retention_tables(L)
    consts = (W["gbias"], W["conv_w"], W["conv_b"], W["gn_m"], W["gn_r"], decay, rtab, cdec)
    m0 = jnp.broadcast_to(st_m[..., None], st_m.shape + (LANES,))
    hcat, C, n, m, conv, S = _even_mixer(
        p.reshape(B, T, PROJ_MAIN_A), gates.reshape(B, T, LANES), jnp.cos(ang), jnp.sin(ang),
        consts, st_C, st_n, m0, st_conv, st_S, TB, L)
    x2 = _out_proj(hcat.reshape(M, D_M + D_R), W["w_out_a"], x2, W["norm_post"][0:1])

    tm = min(M, IN_PROJ_ODD_TM)
    n_rep = max(1, tm // T)
    ra, rb, rc = _partial_rope_tables(jnp.tile(pos, n_rep))
    k_transposed = T % tm == 0
    kt_tables = None
    if k_transposed:
        ang_c = _rope_angles(pos, ROPE_DIM, ROPE_THETA)
        kt_tables = (W["w_k_t"], jnp.cos(ang_c).T, jnp.sin(ang_c).T, T)
    q, kf, kb, vf, vb, z = _in_proj_c(x2, W["norm_pre"][1:2], W["w_in_c"], ra, rb, rc, tm, kt_tables)
    if k_transposed:
        k_out = jnp.transpose(kf.reshape(B, 2 * H_C, HD_C, T), (0, 3, 1, 2))[None]
    else:
        kf = kf.reshape(B, T, D_QKC)
        k_out = kf.reshape(1, B, T, 2 * H_C, HD_C)
    a = attend(q.reshape(B, T, D_QKC), kf, kb.reshape(B, T, D_QKC),
               vf.reshape(B, T, D_C), vb.reshape(B, T, D_C), z.reshape(B, T, D_C))
    y = _out_proj(a.reshape(M, D_C), W["w_out_c"], x2, W["norm_post"][1:2])
    return (y.reshape(B, T, D_MODEL), C[None], n[None], m[None, :, :, 0], conv[None], S[None],
            k_out, vf.reshape(1, B, T, H_C, 2 * HD_C))


def kernel(x_prompt, x_sample, state_mlstm_C, state_mlstm_n, state_mlstm_m, state_mlstm_conv, state_ret_S, cache_k, cache_v, page_table, norm_pre, norm_post, w_in_a, b_gate_i, b_gate_f, conv_w, conv_b, gn_mlstm, gn_ret, w_out_a, w_in_c, lambda_qk, gn_diff, w_out_c):
    g0 = 2 * D_QKM + 2 * D_M
    wa = w_in_a[0]
    W = {
        "norm_pre": norm_pre, "norm_post": norm_post,
        "w_main_a": jnp.concatenate([wa[:, :g0], wa[:, g0 + 2 * H_M:]], axis=1).astype(BF16),
        "w_gate_a": jnp.pad(wa[:, g0:g0 + 2 * H_M], ((0, 0), (0, LANES - 2 * H_M))).astype(BF16),
        "gbias": jnp.pad(jnp.concatenate([b_gate_i[0], b_gate_f[0]])[None], ((0, 0), (0, LANES - 2 * H_M))),
        "conv_w": conv_w[0], "conv_b": conv_b[0][None],
        "gn_m": gn_mlstm[0].reshape(1, D_M), "gn_r": gn_ret[0].reshape(1, D_R),
        "w_out_a": w_out_a[0].astype(BF16),
        "w_in_c": w_in_c[0].astype(BF16),
        "w_k_t": w_in_c[0][:, D_QKC:2 * D_QKC].T.astype(BF16),
        "w_out_c": w_out_c[0].astype(BF16),
    }
    lam_init = 0.8 - 0.6 * math.exp(-0.3 * 1)
    lq = lambda_qk[0]
    gn_d = gn_diff[0]

    Bp, Tp, _ = x_prompt.shape

    def attend_prompt(q, kf, kb, vf, vb, z):
        return _diff_attn_prompt(lq, q, kb, vb, z, gn_d[:, None, :], lam_init, tq=ATTN_TQ,
                                 G=ATTN_HEADS_PER_STEP)

    outs_p = _forward(
        x_prompt, 0,
        jnp.zeros((Bp, H_M, DK_M, DV_M), F32), jnp.zeros((Bp, H_M, DK_M), F32),
        jnp.zeros((Bp, H_M), F32), jnp.zeros((Bp, CONV_W - 1, 2 * D_QKM), F32),
        jnp.zeros((Bp, H_R, DK_R, DV_R), F32), attend_prompt, W)

    Bs, Ts, _ = x_sample.shape
    n_pool, page_size = cache_k.shape[1], cache_k.shape[2]
    past = page_table.shape[1] * page_size
    ck = jnp.transpose(cache_k[0], (0, 2, 3, 1)).reshape(n_pool, D_QKC, page_size)
    cv = cache_v[0].reshape(n_pool, page_size * H_C, 2 * HD_C)

    def attend_sample(q, kf, kb, vf, vb, z):
        head_of_lane = jnp.arange(D_QKC) // HD_C
        sel = (head_of_lane[None, :] == jnp.arange(2 * H_C)[:, None])
        qrows = jnp.where(sel[None, :, None, :], q[:, None, :, :], jnp.zeros((), q.dtype))
        qrows = qrows.reshape(Bs, 2 * H_C * Ts, D_QKC)
        n_new = NEW_TOKEN_ROWS
        kn = jnp.pad(jnp.transpose(kf, (0, 2, 1)), ((0, 0), (0, 0), (0, n_new - Ts)))
        vn = jnp.pad(vf, ((0, 0), (0, n_new - Ts), (0, 0))).reshape(Bs, n_new * H_C, 2 * HD_C)
        tok = jnp.arange(2 * H_C * Ts) % Ts
        col = jnp.arange(n_new)
        bias = jnp.where(col[None, :] <= tok[:, None], 0.0, -jnp.inf).astype(F32)
        return _paged_attn(page_table, lq, qrows, ck, cv, kn, vn, bias, z, gn_d, lam_init,
                           P=PAGES_PER_STEP)

    outs_s = _forward(x_sample, past, state_mlstm_C[0], state_mlstm_n[0], state_mlstm_m[0],
                      state_mlstm_conv[0], state_ret_S[0], attend_sample, W)

    return (outs_p[0], outs_s[0]) + tuple(outs_p[1:]) + tuple(outs_s[1:])
```

```python
import functools
import math

import jax
import jax.numpy as jnp
from jax import lax
from jax.experimental import pallas as pl
from jax.experimental.pallas import tpu as pltpu

F32 = jnp.float32
BF16 = jnp.bfloat16

D_MODEL = 1024
H_M, DK_M, DV_M = 4, 256, 256
D_QKM = H_M * DK_M
D_M = H_M * DV_M
CONV_W = 4
H_R, DK_R, DV_R = 4, 256, 256
D_QKR = H_R * DK_R
D_R = H_R * DV_R
RET_THETA = 10000.0
H_C, HD_C = 8, 64
D_QKC = 2 * H_C * HD_C
D_C = H_C * 2 * HD_C
ROPE_THETA = 500000.0
ROPE_DIM = HD_C // 4
CHUNK = 256
EPS = 1e-6
PROJ_MAIN_A = 2 * D_QKM + 2 * D_M + 2 * D_QKR + 2 * D_R
LANES = 128
LOG2E = math.log2(math.e)
NEW_TOKEN_ROWS = 16
VMEM_LIMIT = 52 * 1024 * 1024
SUBLANES = 8

IN_PROJ_EVEN_TM, IN_PROJ_EVEN_TN = 1024, 2048
IN_PROJ_ODD_TM = 512
OUT_PROJ_TM = 1024
CONV_LANE_GROUP = 256
ATTN_TQ = 256
ATTN_HEADS_PER_STEP = 2
PAGES_PER_STEP = 16

OFF_QKM, OFF_VM, OFF_ZM = 0, 2 * D_QKM, 2 * D_QKM + D_M
OFF_QR = 2 * D_QKM + 2 * D_M
OFF_KR, OFF_VR, OFF_ZR = OFF_QR + D_QKR, OFF_QR + 2 * D_QKR, OFF_QR + 2 * D_QKR + D_R


def _params(sem):
    return pltpu.CompilerParams(dimension_semantics=sem, vmem_limit_bytes=VMEM_LIMIT)


def _rms(x, g):
    return x * lax.rsqrt(jnp.mean(x * x, axis=-1, keepdims=True) + EPS) * g


def _silu(x):
    return x * jax.nn.sigmoid(x)


def _dot(a, b):
    return jnp.dot(a, b, preferred_element_type=F32)


def _dot_nt(a, b):
    return lax.dot_general(a, b, (((1,), (1,)), ((), ())), preferred_element_type=F32)


def _dot_tn(a, b):
    return lax.dot_general(a, b, (((0,), (0,)), ((), ())), preferred_element_type=F32)


def _in_proj_a_kernel(x_ref, g_ref, w_ref, wg_ref, o_ref, og_ref, xn_ref):
    @pl.when(pl.program_id(1) == 0)
    def _():
        xn = _rms(x_ref[...], g_ref[...]).astype(BF16)
        xn_ref[...] = xn
        og_ref[...] = _dot(xn, wg_ref[...])

    o_ref[...] = _dot(xn_ref[...], w_ref[...]).astype(o_ref.dtype)


def _in_proj_a(x, g, w_main, w_gate, out_dtype):
    M, K = x.shape
    N = w_main.shape[1]
    tm = min(M, IN_PROJ_EVEN_TM)
    tn = IN_PROJ_EVEN_TN
    return pl.pallas_call(
        _in_proj_a_kernel,
        grid=(M // tm, N // tn),
        in_specs=[
            pl.BlockSpec((tm, K), lambda i, j: (i, 0)),
            pl.BlockSpec((1, K), lambda i, j: (0, 0)),
            pl.BlockSpec((K, tn), lambda i, j: (0, j)),
            pl.BlockSpec((K, LANES), lambda i, j: (0, 0)),
        ],
        out_specs=[
            pl.BlockSpec((tm, tn), lambda i, j: (i, j)),
            pl.BlockSpec((tm, LANES), lambda i, j: (i, 0)),
        ],
        out_shape=[
            jax.ShapeDtypeStruct((M, N), out_dtype),
            jax.ShapeDtypeStruct((M, LANES), F32),
        ],
        scratch_shapes=[pltpu.VMEM((tm, K), BF16)],
        compiler_params=_params(("parallel", "arbitrary")),
        name="in_proj_even",
    )(x, g, w_main, w_gate)


def _even_mixer_kernel(p_ref, g_ref, cos_ref, sin_ref, gb_ref, cw_ref, cb_ref, gnm_ref, gnr_ref,
                       dec_ref, rtab_ref, cdec_ref, C0_ref, n0_ref, m0_ref, cv0_ref, S0_ref,
                       h_ref, C_ref, n_ref, m_ref, cv_ref, S_ref, xbuf, qkc, qkr, *, TB, L):
    t = pl.program_id(1)
    HIST = SUBLANES

    @pl.when(t == 0)
    def _():
        C_ref[...] = C0_ref[...]
        n_ref[...] = n0_ref[...]
        m_ref[...] = m0_ref[...]
        S_ref[...] = S0_ref[...]
        xbuf[HIST - (CONV_W - 1):HIST, :] = cv0_ref[0]

    mxu_shift = p_ref.dtype == BF16 and TB % 16 == 0
    if mxu_shift:
        ti = lax.broadcasted_iota(jnp.int32, (TB, TB), 0)
        si = lax.broadcasted_iota(jnp.int32, (TB, TB), 1)
        shifts = [jnp.where(ti - si == CONV_W - 1 - d, 1.0, 0.0).astype(BF16) for d in range(CONV_W)]
    else:
        @pl.when(t > 0)
        def _():
            xbuf[0:HIST, :] = xbuf[TB:TB + HIST, :]

    for cg in range(2 * D_QKM // CONV_LANE_GROUP):
        cs = slice(cg * CONV_LANE_GROUP, (cg + 1) * CONV_LANE_GROUP)
        scale = DK_M ** -0.5 if cg < D_QKM // CONV_LANE_GROUP else 1.0
        if mxu_shift:
            xb = p_ref[0, :, cs]
            taps = [_dot(sh, xb) for sh in shifts]
            acc = cb_ref[:, cs] + taps[0] * cw_ref[0:1, cs]
            for d in range(1, CONV_W):
                acc = acc + taps[d] * cw_ref[d:d + 1, cs]
            qkc[0:TB, cs] = _silu(acc) * scale
            xbuf[HIST:2 * HIST, cs] = taps[CONV_W - 1][0:HIST]
            head = cb_ref[:, cs] + xbuf[HIST - 3:2 * HIST - 3, cs] * cw_ref[0:1, cs]
            for d in range(1, CONV_W):
                head = head + xbuf[HIST - 3 + d:2 * HIST - 3 + d, cs] * cw_ref[d:d + 1, cs]
            qkc[0:HIST, cs] = _silu(head) * scale
            xbuf[0:HIST, cs] = taps[CONV_W - 1][TB - HIST:TB]
        else:
            x = p_ref[0, :, cs].astype(F32)
            xbuf[HIST:HIST + TB, cs] = x
            acc = cb_ref[:, cs] + xbuf[HIST - 3:HIST - 3 + TB, cs] * cw_ref[0:1, cs]
            acc = acc + xbuf[HIST - 2:HIST - 2 + TB, cs] * cw_ref[1:2, cs]
            acc = acc + xbuf[HIST - 1:HIST - 1 + TB, cs] * cw_ref[2:3, cs]
            acc = acc + x * cw_ref[3:4, cs]
            qkc[0:TB, cs] = _silu(acc) * scale
    if mxu_shift:
        cv_ref[0] = xbuf[HIST - (CONV_W - 1):HIST, :]
    else:
        cv_ref[0] = xbuf[HIST + TB - (CONV_W - 1):HIST + TB, :]

    cos = cos_ref[...]
    sin = sin_ref[...]
    half = DK_R // 2
    for which, off, scale in ((0, OFF_QR, 1.0), (1, OFF_KR, DK_R ** -0.5)):
        for h in range(H_R):
            x1 = p_ref[0, :, off + h * DK_R:off + h * DK_R + half].astype(F32)
            x2 = p_ref[0, :, off + h * DK_R + half:off + (h + 1) * DK_R].astype(F32)
            base = which * D_QKR + h * DK_R
            qkr[0:TB, base:base + half] = (x1 * cos - x2 * sin) * scale
            qkr[0:TB, base + half:base + DK_R] = (x1 * sin + x2 * cos) * scale

    def chunk(rows):
        gates = g_ref[0, rows, :] + gb_ref[...]
        ti = lax.broadcasted_iota(jnp.int32, (L, L), 0)
        si = lax.broadcasted_iota(jnp.int32, (L, L), 1)
        eye = ti == si
        low = si <= ti
        upp = ti <= si
        for h in range(H_M):
            ig = gates[:, h:h + 1]
            lf = jax.nn.log_sigmoid(gates[:, H_M + h:H_M + h + 1])
            lf_r = jnp.sum(jnp.where(eye, lf, 0.0), axis=0, keepdims=True)
            i_r = jnp.sum(jnp.where(eye, ig, 0.0), axis=0, keepdims=True)
            b_c = jnp.sum(jnp.where(low, lf_r, 0.0), axis=1, keepdims=True)
            b_r = jnp.sum(jnp.where(upp, lf, 0.0), axis=0, keepdims=True)
            m_prev = m_ref[0, h:h + 1, 0:1]
            inter = b_c + m_prev
            Dm = jnp.where(low, b_c - b_r + i_r, -jnp.inf)
            m_t = jnp.maximum(inter, jnp.max(Dm, axis=1, keepdims=True))
            w = jnp.exp(Dm - m_t)
            s_int = jnp.exp(inter - m_t)
            q = qkc[rows, h * DK_M:(h + 1) * DK_M]
            k = qkc[rows, D_QKM + h * DK_M:D_QKM + (h + 1) * DK_M]
            vb = p_ref[0, rows, OFF_VM + h * DV_M:OFF_VM + (h + 1) * DV_M].astype(BF16)
            qb = q.astype(BF16)
            qk = _dot_nt(qb, k.astype(BF16)) * w
            C = C_ref[0, h]
            nvec = n_ref[0, h:h + 1, :]
            num = _dot(qk.astype(BF16), vb) + s_int * _dot(qb, C.astype(BF16))
            den = (jnp.sum(qk, axis=1, keepdims=True)
                   + s_int * jnp.sum(q * nvec, axis=1, keepdims=True))
            hh = num / jnp.maximum(jnp.abs(den), jnp.exp(-m_t))
            y = _rms(hh, gnm_ref[:, h * DV_M:(h + 1) * DV_M])
            z = p_ref[0, rows, OFF_ZM + h * DV_M:OFF_ZM + (h + 1) * DV_M].astype(F32)
            h_ref[0, rows, h * DV_M:(h + 1) * DV_M] = (y * _silu(z)).astype(h_ref.dtype)
            bL = b_c[L - 1:L, :]
            dec = bL - b_c + ig
            m_new = jnp.maximum(bL + m_prev, jnp.max(dec, axis=0, keepdims=True))
            ws = jnp.exp(dec - m_new)
            sc = jnp.exp(bL + m_prev - m_new)
            kw = k * ws
            C_ref[0, h] = sc * C + _dot_tn(kw.astype(BF16), vb)
            n_ref[0, h:h + 1, :] = sc * nvec + jnp.sum(kw, axis=0, keepdims=True)
            m_ref[0, h:h + 1, :] = jnp.broadcast_to(m_new, (1, LANES))
        for h in range(H_R):
            q = qkr[rows, h * DK_R:(h + 1) * DK_R]
            k = qkr[rows, D_QKR + h * DK_R:D_QKR + (h + 1) * DK_R]
            vb = p_ref[0, rows, OFF_VR + h * DV_R:OFF_VR + (h + 1) * DV_R].astype(BF16)
            qk = _dot_nt(q.astype(BF16), k.astype(BF16)) * dec_ref[h]
            S = S_ref[0, h]
            q_in = rtab_ref[:, h:h + 1]
            k_out = rtab_ref[:, H_R + h:H_R + h + 1]
            o = _dot(qk.astype(BF16), vb) + _dot((q * q_in).astype(BF16), S.astype(BF16))
            S_ref[0, h] = cdec_ref[:, h:h + 1] * S + _dot_tn((k * k_out).astype(BF16), vb)
            y = _rms(o, gnr_ref[:, h * DV_R:(h + 1) * DV_R])
            z = p_ref[0, rows, OFF_ZR + h * DV_R:OFF_ZR + (h + 1) * DV_R].astype(F32)
            h_ref[0, rows, D_M + h * DV_R:D_M + (h + 1) * DV_R] = (y * _silu(z)).astype(h_ref.dtype)

    n_chunks = TB // L
    if n_chunks == 1:
        chunk(slice(0, L))
    else:
        def body(c, carry):
            chunk(pl.ds(pl.multiple_of(c * L, L), L))
            return carry
        lax.fori_loop(0, n_chunks, body, 0)


def _even_mixer(p, gates, cos_r, sin_r, consts, C0, n0, m0, cv0, S0, TB, L):
    B, T, _ = p.shape
    gbias, cw, cb, gnm, gnr, decay, rtab, cdec = consts
    nt = T // TB
    tb8 = -(-TB // 8) * 8
    full = lambda shape: pl.BlockSpec(shape, lambda b, t: (0,) * len(shape))
    per_b = lambda shape: pl.BlockSpec((1,) + shape, lambda b, t: (b,) + (0,) * len(shape))
    kern = functools.partial(_even_mixer_kernel, TB=TB, L=L)
    return pl.pallas_call(
        kern,
        grid=(B, nt),
        in_specs=[
            pl.BlockSpec((1, TB, PROJ_MAIN_A), lambda b, t: (b, t, 0)),
            pl.BlockSpec((1, TB, LANES), lambda b, t: (b, t, 0)),
            pl.BlockSpec((TB, LANES), lambda b, t: (t, 0)),
            pl.BlockSpec((TB, LANES), lambda b, t: (t, 0)),
            full((1, LANES)), full((CONV_W, 2 * D_QKM)), full((1, 2 * D_QKM)),
            full((1, D_M)), full((1, D_R)), full((H_R, L, L)), full((L, LANES)), full((1, LANES)),
            per_b((H_M, DK_M, DV_M)), per_b((H_M, DK_M)), per_b((H_M, LANES)),
            per_b((CONV_W - 1, 2 * D_QKM)), per_b((H_R, DK_R, DV_R)),
        ],
        out_specs=[
            pl.BlockSpec((1, TB, D_M + D_R), lambda b, t: (b, t, 0)),
            per_b((H_M, DK_M, DV_M)), per_b((H_M, DK_M)), per_b((H_M, LANES)),
            per_b((CONV_W - 1, 2 * D_QKM)), per_b((H_R, DK_R, DV_R)),
        ],
        out_shape=[
            jax.ShapeDtypeStruct((B, T, D_M + D_R), BF16),
            jax.ShapeDtypeStruct((B, H_M, DK_M, DV_M), F32),
            jax.ShapeDtypeStruct((B, H_M, DK_M), F32),
            jax.ShapeDtypeStruct((B, H_M, LANES), F32),
            jax.ShapeDtypeStruct((B, CONV_W - 1, 2 * D_QKM), F32),
            jax.ShapeDtypeStruct((B, H_R, DK_R, DV_R), F32),
        ],
        scratch_shapes=[
            pltpu.VMEM((tb8 + 8, 2 * D_QKM), F32),
            pltpu.VMEM((tb8, 2 * D_QKM), F32),
            pltpu.VMEM((tb8, 2 * D_QKR), F32),
        ],
        compiler_params=_params(("parallel", "arbitrary")),
        name="even_mixer",
    )(p, gates, cos_r, sin_r, gbias, cw, cb, gnm, gnr, decay, rtab, cdec, C0, n0, m0, cv0, S0)


def _out_proj_kernel(h_ref, w_ref, x_ref, g_ref, y_ref):
    o = _dot(h_ref[...], w_ref[...])
    y_ref[...] = x_ref[...] + _rms(o, g_ref[...])


def _out_proj(h, w, x, g):
    M, K = h.shape
    N = w.shape[1]
    tm = min(M, OUT_PROJ_TM)
    return pl.pallas_call(
        _out_proj_kernel,
        grid=(M // tm,),
        in_specs=[
            pl.BlockSpec((tm, K), lambda i: (i, 0)),
            pl.BlockSpec((K, N), lambda i: (0, 0)),
            pl.BlockSpec((tm, N), lambda i: (i, 0)),
            pl.BlockSpec((1, N), lambda i: (0, 0)),
        ],
        out_specs=pl.BlockSpec((tm, N), lambda i: (i, 0)),
        out_shape=jax.ShapeDtypeStruct((M, N), F32),
        compiler_params=_params(("parallel",)),
        name="out_proj",
    )(h, w, x, g)


def _in_proj_c_kernel(x_ref, g_ref, w_ref, ra_ref, rb_ref, rc_ref, *rest, k_transposed):
    if k_transposed:
        wkt_ref, cos_t_ref, sin_t_ref, q_ref, kf_ref, kb_ref, vf_ref, vb_ref, z_ref = rest
    else:
        q_ref, kf_ref, kb_ref, vf_ref, vb_ref, z_ref = rest
    xn = _rms(x_ref[...], g_ref[...]).astype(BF16)
    ra = ra_ref[...]
    rb = rb_ref[...]
    rc = rc_ref[...]
    half = ROPE_DIM // 2

    def rope(y):
        outs = []
        for gi in range(y.shape[1] // LANES):
            yg = y[:, gi * LANES:(gi + 1) * LANES]
            outs.append(yg * ra + pltpu.roll(yg, half, 1) * rb + pltpu.roll(yg, LANES - half, 1) * rc)
        return jnp.concatenate(outs, axis=1)

    q = rope(_dot(xn, w_ref[:, 0:D_QKC]))
    q_ref[...] = (q * (HD_C ** -0.5 * LOG2E)).astype(BF16)
    k = rope(_dot(xn, w_ref[:, D_QKC:2 * D_QKC]))
    kb_ref[...] = k.astype(BF16)
    if k_transposed:
        kt = _dot_nt(wkt_ref[...], xn)
        cos_t = cos_t_ref[...]
        sin_t = sin_t_ref[...]
        pieces = []
        for h in range(2 * H_C):
            x1 = kt[h * HD_C:h * HD_C + half]
            x2 = kt[h * HD_C + half:h * HD_C + ROPE_DIM]
            pieces += [x1 * cos_t - x2 * sin_t, x1 * sin_t + x2 * cos_t, kt[h * HD_C + ROPE_DIM:(h + 1) * HD_C]]
        kf_ref[0] = jnp.concatenate(pieces, axis=0)
    else:
        kf_ref[...] = k
    v = _dot(xn, w_ref[:, 2 * D_QKC:2 * D_QKC + D_C])
    vf_ref[...] = v
    vb_ref[...] = v.astype(BF16)
    z_ref[...] = _dot(xn, w_ref[:, 2 * D_QKC + D_C:]).astype(BF16)


def _in_proj_c(x, g, w, ra, rb, rc, tm, kt_tables=None):
    M, K = x.shape
    n_rt = ra.shape[0] // tm
    row = lambda n: pl.BlockSpec((tm, n), lambda i: (i, 0))
    tab = pl.BlockSpec((tm, LANES), lambda i: (i % n_rt, 0))
    in_specs = [row(K), pl.BlockSpec((1, K), lambda i: (0, 0)), pl.BlockSpec(w.shape, lambda i: (0, 0)),
                tab, tab, tab]
    args = [x, g, w, ra, rb, rc]
    kf_spec, kf_shape = row(D_QKC), jax.ShapeDtypeStruct((M, D_QKC), F32)
    if kt_tables is not None:
        wkt, cos_t, sin_t, T = kt_tables
        tab_t = pl.BlockSpec((cos_t.shape[0], tm), lambda i: (0, i % n_rt))
        in_specs += [pl.BlockSpec(wkt.shape, lambda i: (0, 0)), tab_t, tab_t]
        args += [wkt, cos_t, sin_t]
        kf_spec = pl.BlockSpec((1, D_QKC, tm), lambda i: (i // n_rt, 0, i % n_rt))
        kf_shape = jax.ShapeDtypeStruct((M // T, D_QKC, T), F32)
    return pl.pallas_call(
        functools.partial(_in_proj_c_kernel, k_transposed=kt_tables is not None),
        grid=(M // tm,),
        in_specs=in_specs,
        out_specs=[row(D_QKC), kf_spec, row(D_QKC), row(D_C), row(D_C), row(D_C)],
        out_shape=[
            jax.ShapeDtypeStruct((M, D_QKC), BF16),
            kf_shape,
            jax.ShapeDtypeStruct((M, D_QKC), BF16),
            jax.ShapeDtypeStruct((M, D_C), F32),
            jax.ShapeDtypeStruct((M, D_C), BF16),
            jax.ShapeDtypeStruct((M, D_C), BF16),
        ],
        compiler_params=_params(("parallel",)),
        name="in_proj_odd",
    )(*args)


def _lambda_value(lq_ref, lam_init):
    lf = lq_ref[...]
    l1 = jnp.sum(lf[0:1] * lf[1:2], axis=1, keepdims=True)
    l2 = jnp.sum(lf[2:3] * lf[3:4], axis=1, keepdims=True)
    return jnp.exp(l1) - jnp.exp(l2) + lam_init


def _diff_attn_kernel(it_ref, jt_ref, lq_ref, q_ref, k_ref, v_ref, z_ref, gn_ref, bias_ref, o_ref,
                      vt_sc, q2_sc, s_sc, m_sc, l_sc, acc_sc, *, tq, n_steps, lam_init, G):
    nb = v_ref.shape[1] // tq
    lane = lax.broadcasted_iota(jnp.int32, (tq, LANES), 1)

    def blk(idx):
        return pl.ds(pl.multiple_of(idx * tq, tq), tq)

    def head(g):
        return slice(g * LANES, (g + 1) * LANES)

    for g in range(G):
        for jb in range(nb):
            rows = slice(jb * tq, (jb + 1) * tq)
            vt_sc[g, jb] = v_ref[0, rows, head(g)].astype(F32).T.astype(BF16)
            q = q_ref[0, rows, head(g)]
            zero = jnp.zeros_like(q)
            q2 = jnp.concatenate([jnp.where(lane < HD_C, q, zero), jnp.where(lane >= HD_C, q, zero)], axis=0)
            q2_sc[g, jb] = q2.astype(F32).T.astype(BF16)

    def scores(t, slot, first=False):
        for g in range(G):
            if first:
                s_sc[slot, g] = _dot(k_ref[0, 0:tq, head(g)], q2_sc[g, 0])
            else:
                s_sc[slot, g] = _dot(k_ref[0, blk(jt_ref[t]), head(g)], q2_sc[g, it_ref[t]])

    scores(0, 0, first=True)

    def step(t, slot):
        i = it_ref[t]
        j = jt_ref[t]

        @pl.when(j == 0)
        def _():
            m_sc[...] = jnp.full(m_sc.shape, -jnp.inf, F32)
            l_sc[...] = jnp.zeros(l_sc.shape, F32)
            acc_sc[...] = jnp.zeros(acc_sc.shape, F32)

        scores(t + 1, 1 - slot)
        bias = bias_ref[(j == i).astype(jnp.int32)]
        for g in range(G):
            s = s_sc[slot, g] + bias
            m_prev = m_sc[g]
            m_new = jnp.maximum(m_prev, jnp.max(s, axis=0, keepdims=True))
            p = jnp.exp2(s - m_new)
            alpha = jnp.exp2(m_prev - m_new)
            l_sc[g] = alpha * l_sc[g] + jnp.sum(p, axis=0, keepdims=True)
            acc_sc[g] = alpha * acc_sc[g] + _dot(vt_sc[g, j], p.astype(BF16))
            m_sc[g] = m_new

        @pl.when(j == i)
        def _():
            lam = _lambda_value(lq_ref, lam_init)
            for g in range(G):
                out_t = acc_sc[g] / l_sc[g]
                a_t = out_t[:, 0:tq] - lam * out_t[:, tq:2 * tq]
                a_t = a_t * lax.rsqrt(jnp.mean(a_t * a_t, axis=0, keepdims=True) + EPS)
                y = a_t.T * gn_ref[g] * (1.0 - lam_init)
                z = z_ref[0, blk(i), head(g)].astype(F32)
                o_ref[0, blk(i), head(g)] = (y * _silu(z)).astype(o_ref.dtype)

    def body(u, carry):
        step(2 * u, 0)
        step(2 * u + 1, 1)
        return carry

    assert n_steps % 2 == 0
    lax.fori_loop(0, n_steps // 2, body, 0)


def _diff_attn_prompt(lq, q, k, v, z, gn, lam_init, tq, G):
    B, T, _ = q.shape
    nq = T // tq
    pairs = [(i, j) for i in range(nq) for j in range(i + 1)]
    n_steps = len(pairs)
    pairs.append((0, 0))
    i_tab = jnp.asarray([p[0] for p in pairs], jnp.int32)
    j_tab = jnp.asarray([p[1] for p in pairs], jnp.int32)
    qpos = jnp.arange(2 * tq) % tq
    causal = jnp.where(jnp.arange(tq)[:, None] <= qpos[None, :], 0.0, -jnp.inf).astype(F32)
    bias = jnp.stack([jnp.zeros_like(causal), causal])
    kern = functools.partial(_diff_attn_kernel, tq=tq, n_steps=n_steps, lam_init=lam_init, G=G)
    seq = pl.BlockSpec((1, T, G * LANES), lambda b, h, it, jt: (b, 0, h))
    grid_spec = pltpu.PrefetchScalarGridSpec(
        num_scalar_prefetch=2,
        grid=(B, H_C // G),
        in_specs=[
            pl.BlockSpec(lq.shape, lambda b, h, it, jt: (0, 0)),
            seq, seq, seq, seq,
            pl.BlockSpec((G, 1, LANES), lambda b, h, it, jt: (h, 0, 0)),
            pl.BlockSpec(bias.shape, lambda b, h, it, jt: (0, 0, 0)),
        ],
        out_specs=seq,
        scratch_shapes=[
            pltpu.VMEM((G, nq, LANES, tq), BF16),
            pltpu.VMEM((G, nq, LANES, 2 * tq), BF16),
            pltpu.VMEM((2, G, tq, 2 * tq), F32),
            pltpu.VMEM((G, 1, 2 * tq), F32),
            pltpu.VMEM((G, 1, 2 * tq), F32),
            pltpu.VMEM((G, LANES, 2 * tq), F32),
        ],
    )
    return pl.pallas_call(
        kern,
        grid_spec=grid_spec,
        out_shape=jax.ShapeDtypeStruct((B, T, D_C), BF16),
        compiler_params=_params(("parallel", "parallel")),
        name="diff_attn_prompt",
    )(i_tab, j_tab, lq, q, k, v, z, gn, bias)


def _paged_attn_kernel(pt_ref, lq_ref, qr_ref, *rest, P, lam_init, Tn):
    k_refs = rest[:P]
    v_refs = rest[P:2 * P]
    kn_ref, vn_ref, bias_ref, z_ref, gn_ref, o_ref, m_sc, l_sc, acc_sc = rest[2 * P:]
    j = pl.program_id(1)
    GR = 2 * Tn

    @pl.when(j == 0)
    def _():
        m_sc[...] = jnp.full(m_sc.shape, -jnp.inf, F32)
        l_sc[...] = jnp.zeros(l_sc.shape, F32)
        acc_sc[...] = jnp.zeros(acc_sc.shape, F32)

    qr = qr_ref[0]

    def pages(kt_refs, vp_refs, bias):
        tiles = [_dot(qr, kt_ref[0].astype(BF16)) for kt_ref in kt_refs]
        if bias is not None:
            tiles = [x + bias for x in tiles]
        m_prev = m_sc[...]
        m_new = jnp.maximum(m_prev, jnp.max(functools.reduce(jnp.maximum, tiles), axis=1, keepdims=True))
        alpha = jnp.exp2(m_prev - m_new)
        m_tile = m_new[:, 0:tiles[0].shape[1]]
        ps = [jnp.exp2(x - m_tile) for x in tiles]
        l_sc[...] = alpha * l_sc[...] + jnp.sum(functools.reduce(jnp.add, ps), axis=1, keepdims=True)
        m_sc[...] = m_new
        for hp in range(H_C):
            rs = slice(hp * GR, (hp + 1) * GR)
            p_h = jnp.concatenate([x[rs, :].astype(BF16) for x in ps], axis=1)
            v_h = jnp.concatenate(
                [vp_ref[0, pl.ds(hp, vp_ref.shape[1] // H_C, stride=H_C), :].astype(BF16) for vp_ref in vp_refs],
                axis=0)
            acc_sc[rs, :] = alpha[rs, :] * acc_sc[rs, :] + _dot(p_h, v_h)

    pages(k_refs, v_refs, None)

    @pl.when(j == pl.num_programs(1) - 1)
    def _():
        pages([kn_ref], [vn_ref], bias_ref[...])
        out = acc_sc[...] / l_sc[...]
        lam = _lambda_value(lq_ref, lam_init)
        for hp in range(H_C):
            a = out[hp * GR:hp * GR + Tn] - lam * out[hp * GR + Tn:(hp + 1) * GR]
            y = _rms(a, gn_ref[hp:hp + 1, :]) * (1.0 - lam_init)
            z = z_ref[0, :, hp * LANES:(hp + 1) * LANES].astype(F32)
            o_ref[0, :, hp * LANES:(hp + 1) * LANES] = (y * _silu(z)).astype(o_ref.dtype)


def _paged_attn(page_table, lq, qrows, ck, cv, kn, vn, bias, z, gn, lam_init, P):
    B, R, _ = qrows.shape
    n_pages = page_table.shape[1]
    Tn = z.shape[1]
    kern = functools.partial(_paged_attn_kernel, P=P, lam_init=lam_init, Tn=Tn)

    def page_spec(arr, pi):
        return pl.BlockSpec((1,) + arr.shape[1:], lambda b, j, pt: (pt[b, j * P + pi], 0, 0))

    per_b = lambda shape: pl.BlockSpec((1,) + shape, lambda b, j, pt: (b,) + (0,) * len(shape))
    full = lambda shape: pl.BlockSpec(shape, lambda b, j, pt: (0,) * len(shape))
    grid_spec = pltpu.PrefetchScalarGridSpec(
        num_scalar_prefetch=1,
        grid=(B, n_pages // P),
        in_specs=[full(lq.shape), per_b(qrows.shape[1:])]
        + [page_spec(ck, pi) for pi in range(P)] + [page_spec(cv, pi) for pi in range(P)]
        + [per_b(kn.shape[1:]), per_b(vn.shape[1:]), full(bias.shape), per_b(z.shape[1:]), full(gn.shape)],
        out_specs=per_b((Tn, D_C)),
        scratch_shapes=[pltpu.VMEM((R, LANES), F32)] * 3,
    )
    return pl.pallas_call(
        kern,
        grid_spec=grid_spec,
        out_shape=jax.ShapeDtypeStruct((B, Tn, D_C), BF16),
        compiler_params=_params(("parallel", "arbitrary")),
        name="paged_diff_attn",
    )(page_table, lq, qrows, *([ck] * P), *([cv] * P), kn, vn, bias, z, gn)


def _rope_angles(pos, rot_dim, theta):
    half = rot_dim // 2
    inv = jnp.power(theta, -jnp.arange(half, dtype=F32) / half)
    return pos.astype(F32)[:, None] * inv[None, :]


def _retention_tables(L):
    lg = jnp.log(1.0 - jnp.power(2.0, -5.0 - jnp.arange(H_R, dtype=F32)))
    j = jnp.arange(L, dtype=F32)
    rel = j[:, None] - j[None, :]
    decay = jnp.where((rel >= 0)[None], jnp.exp(rel[None] * lg[:, None, None]), 0.0)
    q_in = jnp.exp((j + 1.0)[:, None] * lg)
    k_out = jnp.exp((L - 1.0 - j)[:, None] * lg)
    rtab = jnp.zeros((L, LANES), F32).at[:, 0:H_R].set(q_in).at[:, H_R:2 * H_R].set(k_out)
    cdec = jnp.zeros((1, LANES), F32).at[0, 0:H_R].set(jnp.exp(L * lg))
    return decay, rtab, cdec


def _partial_rope_tables(pos):
    ang = _rope_angles(pos, ROPE_DIM, ROPE_THETA)
    half = ROPE_DIM // 2
    cos, sin = jnp.cos(ang), jnp.sin(ang)
    T = pos.shape[0]
    ra = jnp.ones((T, HD_C), F32).at[:, 0:half].set(cos).at[:, half:ROPE_DIM].set(cos)
    rb = jnp.zeros((T, HD_C), F32).at[:, half:ROPE_DIM].set(sin)
    rc = jnp.zeros((T, HD_C), F32).at[:, 0:half].set(-sin)
    rep = LANES // HD_C
    return jnp.tile(ra, (1, rep)), jnp.tile(rb, (1, rep)), jnp.tile(rc, (1, rep))


def _forward(x, pos0, st_C, st_n, st_m, st_conv, st_S, attend, W):
    B, T, _ = x.shape
    M = B * T
    act_dtype = BF16 if T >= CHUNK else F32
    pos = pos0 + jnp.arange(T)

    x2 = x.reshape(M, D_MODEL)
    p, gates = _in_proj_a(x2, W["norm_pre"][0:1], W["w_main_a"], W["w_gate_a"], act_dtype)
    L = CHUNK if T % CHUNK == 0 else T
    TB = L
    ang = _rope_angles(pos, DK_R, RET_THETA)
    decay, rtab, cdec = _retention_tables(L)
    consts = (W["gbias"], W["conv_w"], W["conv_b"], W["gn_m"], W["gn_r"], decay, rtab, cdec)
    m0 = jnp.broadcast_to(st_m[..., None], st_m.shape + (LANES,))
    hcat, C, n, m, conv, S = _even_mixer(
        p.reshape(B, T, PROJ_MAIN_A), gates.reshape(B, T, LANES), jnp.cos(ang), jnp.sin(ang),
        consts, st_C, st_n, m0, st_conv, st_S, TB, L)
    x2 = _out_proj(hcat.reshape(M, D_M + D_R), W["w_out_a"], x2, W["norm_post"][0:1])

    tm = min(M, IN_PROJ_ODD_TM)
    n_rep = max(1, tm // T)
    ra, rb, rc = _partial_rope_tables(jnp.tile(pos, n_rep))
    k_transposed = T % tm == 0
    kt_tables = None
    if k_transposed:
        ang_c = _rope_angles(pos, ROPE_DIM, ROPE_THETA)
        kt_tables = (W["w_k_t"], jnp.cos(ang_c).T, jnp.sin(ang_c).T, T)
    q, kf, kb, vf, vb, z = _in_proj_c(x2, W["norm_pre"][1:2], W["w_in_c"], ra, rb, rc, tm, kt_tables)
    if k_transposed:
        k_out = jnp.transpose(kf.reshape(B, 2 * H_C, HD_C, T), (0, 3, 1, 2))[None]
    else:
        kf = kf.reshape(B, T, D_QKC)
        k_out = kf.reshape(1, B, T, 2 * H_C, HD_C)
    a = attend(q.reshape(B, T, D_QKC), kf, kb.reshape(B, T, D_QKC),
               vf.reshape(B, T, D_C), vb.reshape(B, T, D_C), z.reshape(B, T, D_C))
    y = _out_proj(a.reshape(M, D_C), W["w_out_c"], x2, W["norm_post"][1:2])
    return (y.reshape(B, T, D_MODEL), C[None], n[None], m[None, :, :, 0], conv[None], S[None],
            k_out, vf.reshape(1, B, T, H_C, 2 * HD_C))


def kernel(x_prompt, x_sample, state_mlstm_C, state_mlstm_n, state_mlstm_m, state_mlstm_conv, state_ret_S, cache_k, cache_v, page_table, norm_pre, norm_post, w_in_a, b_gate_i, b_gate_f, conv_w, conv_b, gn_mlstm, gn_ret, w_out_a, w_in_c, lambda_qk, gn_diff, w_out_c):
    g0 = 2 * D_QKM + 2 * D_M
    wa = w_in_a[0]
    W = {
        "norm_pre": norm_pre, "norm_post": norm_post,
        "w_main_a": jnp.concatenate([wa[:, :g0], wa[:, g0 + 2 * H_M:]], axis=1).astype(BF16),
        "w_gate_a": jnp.pad(wa[:, g0:g0 + 2 * H_M], ((0, 0), (0, LANES - 2 * H_M))).astype(BF16),
        "gbias": jnp.pad(jnp.concatenate([b_gate_i[0], b_gate_f[0]])[None], ((0, 0), (0, LANES - 2 * H_M))),
        "conv_w": conv_w[0], "conv_b": conv_b[0][None],
        "gn_m": gn_mlstm[0].reshape(1, D_M), "gn_r": gn_ret[0].reshape(1, D_R),
        "w_out_a": w_out_a[0].astype(BF16),
        "w_in_c": w_in_c[0].astype(BF16),
        "w_k_t": w_in_c[0][:, D_QKC:2 * D_QKC].T.astype(BF16),
        "w_out_c": w_out_c[0].astype(BF16),
    }
    lam_init = 0.8 - 0.6 * math.exp(-0.3 * 1)
    lq = lambda_qk[0]
    gn_d = gn_diff[0]

    Bp, Tp, _ = x_prompt.shape

    def attend_prompt(q, kf, kb, vf, vb, z):
        return _diff_attn_prompt(lq, q, kb, vb, z, gn_d[:, None, :], lam_init, tq=ATTN_TQ,
                                 G=ATTN_HEADS_PER_STEP)

    outs_p = _forward(
        x_prompt, 0,
        jnp.zeros((Bp, H_M, DK_M, DV_M), F32), jnp.zeros((Bp, H_M, DK_M), F32),
        jnp.zeros((Bp, H_M), F32), jnp.zeros((Bp, CONV_W - 1, 2 * D_QKM), F32),
        jnp.zeros((Bp, H_R, DK_R, DV_R), F32), attend_prompt, W)

    Bs, Ts, _ = x_sample.shape
    n_pool, page_size = cache_k.shape[1], cache_k.shape[2]
    past = page_table.shape[1] * page_size
    ck = jnp.transpose(cache_k[0], (0, 2, 3, 1)).reshape(n_pool, D_QKC, page_size)
    cv = cache_v[0].reshape(n_pool, page_size * H_C, 2 * HD_C)

    def attend_sample(q, kf, kb, vf, vb, z):
        head_of_lane = jnp.arange(D_QKC) // HD_C
        sel = (head_of_lane[None, :] == jnp.arange(2 * H_C)[:, None])
        qrows = jnp.where(sel[None, :, None, :], q[:, None, :, :], jnp.zeros((), q.dtype))
        qrows = qrows.reshape(Bs, 2 * H_C * Ts, D_QKC)
        n_new = NEW_TOKEN_ROWS
        kn = jnp.pad(jnp.transpose(kf, (0, 2, 1)), ((0, 0), (0, 0), (0, n_new - Ts)))
        vn = jnp.pad(vf, ((0, 0), (0, n_new - Ts), (0, 0))).reshape(Bs, n_new * H_C, 2 * HD_C)
        tok = jnp.arange(2 * H_C * Ts) % Ts
        col = jnp.arange(n_new)
        bias = jnp.where(col[None, :] <= tok[:, None], 0.0, -jnp.inf).astype(F32)
        return _paged_attn(page_table, lq, qrows, ck, cv, kn, vn, bias, z, gn_d, lam_init,
                           P=PAGES_PER_STEP)

    outs_s = _forward(x_sample, past, state_mlstm_C[0], state_mlstm_n[0], state_mlstm_m[0],
                      state_mlstm_conv[0], state_ret_S[0], attend_sample, W)

    return (outs_p[0], outs_s[0]) + tuple(outs_p[1:]) + tuple(outs_s[1:])
```
